```python
import jax, jax.numpy as jnp
from jax import lax
import numpy as np

D_MODEL = 4096
BATCH = 1
SEQ = 8192
DEPTH = 4

CHUNK = 64
MIX_W = D_MODEL
A_W = MIX_W // 4
A_HEADS = 4
A_DV = A_W // A_HEADS
A_DK = A_DV // 2
A_QK = A_HEADS * A_DK
A_IN = 2 * A_QK + 2 * A_W
ROPE_BASE = 10000.0
A_NORM_EPS = 1e-6
B_W = MIX_W // 4
B_HEAD = 64
B_HEADS = B_W // B_HEAD
B_DECAY_RANK = 128
B_AAA_RANK = 128
B_SHIFT_W = 3 * B_W + B_DECAY_RANK + B_AAA_RANK
B_LN_EPS = 64e-5
C_W = MIX_W // 2
C_HEADS = 16
C_DH = C_W // C_HEADS
C_BACK = 8
C_REL_CLIP = 128
C_N_REL = (CHUNK - 1) + C_REL_CLIP + 1
C_IN = 4 * C_W
IN_W = A_IN + B_SHIFT_W + B_W + C_IN
RMS_EPS = 1e-6

kernel_name = "hybrid_retention_rwkv7_chunkattn_trunk"


def _split(z, sizes):
    return jnp.split(z, np.cumsum(np.array(sizes))[:-1].tolist(), axis=-1)


def rms_norm(x, g):
    xf = x.astype(jnp.float32)
    y = xf * lax.rsqrt(jnp.mean(xf * xf, axis=-1, keepdims=True) + RMS_EPS)
    return (y * g.astype(jnp.float32)).astype(x.dtype)


def rotary(x, pos):
    d = x.shape[-1]
    inv = 1.0 / (ROPE_BASE ** (jnp.arange(0, d, 2, dtype=jnp.float32) / d))
    ang = pos.astype(jnp.float32)[:, None] * inv[None, :]
    cos = jnp.cos(ang)[None, :, None, :]
    sin = jnp.sin(ang)[None, :, None, :]
    xf = x.astype(jnp.float32)
    x1, x2 = xf[..., 0::2], xf[..., 1::2]
    out = jnp.stack([x1 * cos - x2 * sin, x1 * sin + x2 * cos], axis=-1)
    return out.reshape(x.shape).astype(x.dtype)


def retention(q, k, v):
    B_, S_, H, dk = q.shape
    dv = v.shape[-1]
    nc = S_ // CHUNK
    f32 = jnp.float32
    log_g = jnp.log(1.0 - 2.0 ** (-5.0 - jnp.arange(H, dtype=f32)))
    idx = jnp.arange(CHUNK, dtype=f32)
    intra = jnp.exp(log_g[:, None, None] * jnp.abs(idx[:, None] - idx[None, :]))
    to_end = jnp.exp(log_g[:, None] * (CHUNK - 1 - idx)[None, :])
    from_start = jnp.exp(log_g[:, None] * (idx + 1.0)[None, :])
    chunk_decay = jnp.exp(log_g * CHUNK)
    qf = q.astype(f32).reshape(B_, nc, CHUNK, H, dk)
    kf = k.astype(f32).reshape(B_, nc, CHUNK, H, dk) * (dk ** -0.5)
    vf = v.astype(f32).reshape(B_, nc, CHUNK, H, dv)
    scores = jnp.einsum('bcnhd,bcmhd->bchnm', qf, kf) * intra
    o_intra = jnp.einsum('bchnm,bcmhe->bcnhe', scores, vf)
    u = jnp.einsum('bcmhd,bcmhe,hm->cbhde', kf, vf, to_end)

    def step(s, u_c):
        return chunk_decay[None, :, None, None] * s + u_c, s

    _, s_prev = lax.scan(step, jnp.zeros((B_, H, dk, dv), f32), u)
    o_inter = jnp.einsum('bcnhd,cbhde,hn->bcnhe', qf, s_prev, from_start)
    o = (o_intra + o_inter).reshape(B_, S_, H, dv)
    o = o * lax.rsqrt(jnp.mean(o * o, axis=-1, keepdims=True) + A_NORM_EPS)
    return o.reshape(B_, S_, H * dv).astype(q.dtype)


def token_shift(z):
    return jnp.pad(z, ((0, 0), (1, 0), (0, 0)))[:, :-1]


def _wkv7_step(s, inp):
    r_t, w_t, k_t, v_t, kk_t, a_t = inp
    sa = jnp.einsum('bhij,bhj->bhi', s, kk_t)
    s = (s * w_t[:, :, None, :]
         - sa[..., :, None] * (kk_t * a_t)[..., None, :]
         + v_t[..., :, None] * k_t[..., None, :])
    y = jnp.einsum('bhij,bhj->bhi', s, r_t)
    return s, y


def rwkv7_time_mix(hb, mu, w0, w_up, a0, a_up, k_k, k_a, r_k, ln_g, ln_b):
    B_, S_, _ = hb.shape
    f32 = jnp.float32
    hb = hb + (token_shift(hb) - hb) * mu
    r, k, v, wd, ad = _split(hb, (B_W, B_W, B_W, B_DECAY_RANK, B_AAA_RANK))
    log_w = -jnp.exp(-jax.nn.softplus(-(w0 + jnp.tanh(wd) @ w_up).astype(f32)) - 0.5)
    a = jax.nn.sigmoid((a0 + ad @ a_up).astype(f32))

    def heads(z):
        return z.astype(f32).reshape(B_, S_, B_HEADS, B_HEAD)

    kk = heads(k * k_k)
    kk = kk / jnp.maximum(jnp.sqrt(jnp.sum(kk * kk, axis=-1, keepdims=True)), 1e-12)
    k_mod = k.astype(f32) * (1.0 + (a - 1.0) * k_a.astype(f32))
    r_h, k_h, v_h, a_h, w_h = heads(r), heads(k_mod), heads(v), heads(a), heads(jnp.exp(log_w))
    xs = tuple(jnp.moveaxis(z, 1, 0) for z in (r_h, w_h, k_h, v_h, kk, a_h))
    s0 = jnp.zeros((B_, B_HEADS, B_HEAD, B_HEAD), f32)
    _, y = lax.scan(_wkv7_step, s0, xs)
    y = jnp.moveaxis(y, 0, 1)
    mean = jnp.mean(y, axis=-1, keepdims=True)
    var = jnp.mean((y - mean) ** 2, axis=-1, keepdims=True)
    y = ((y - mean) * lax.rsqrt(var + B_LN_EPS)).reshape(B_, S_, B_W)
    y = y * ln_g.astype(f32) + ln_b.astype(f32)
    bonus = jnp.sum(r_h * k_h * r_k.astype(f32), axis=-1, keepdims=True) * v_h
    return (y + bonus.reshape(B_, S_, B_W)).astype(hb.dtype)


def chunk_rel_attention(q, k, v, rel_bias):
    B_, S_, H, dh = q.shape
    nc = S_ // CHUNK
    band = C_BACK + 1
    qc = q.reshape(B_, nc, CHUNK, H, dh) * (dh ** -0.5)
    pad = ((0, 0), (C_BACK, 0), (0, 0), (0, 0), (0, 0))
    kc = jnp.pad(k.reshape(B_, nc, CHUNK, H, dh), pad)
    vc = jnp.pad(v.reshape(B_, nc, CHUNK, H, dh), pad)
    cidx = jnp.arange(nc)[:, None] + jnp.arange(band)[None, :]
    kb = kc[:, cidx].reshape(B_, nc, band * CHUNK, H, dh)
    vb = vc[:, cidx].reshape(B_, nc, band * CHUNK, H, dh)
    n = jnp.arange(CHUNK)
    j = jnp.arange(band)
    dist = (n[:, None, None] + (C_BACK - j)[None, :, None] * CHUNK
            - n[None, None, :]).reshape(CHUNK, band * CHUNK)
    rel_idx = jnp.clip(dist, -(CHUNK - 1), C_REL_CLIP) + (CHUNK - 1)
    bias = rel_bias[:, rel_idx].astype(jnp.float32)
    valid = (jnp.arange(nc)[:, None] - C_BACK + j[None, :]) >= 0
    valid = jnp.repeat(valid, CHUNK, axis=1)
    s = jnp.einsum('bcqhd,bckhd->bchqk', qc, kb).astype(jnp.float32) + bias[None, None]
    s = jnp.where(valid[None, :, None, None, :], s, -1e30)
    p = jax.nn.softmax(s, axis=-1).astype(v.dtype)
    o = jnp.einsum('bchqk,bckhd->bcqhd', p, vb)
    return o.reshape(B_, S_, H * dh)


def setup_inputs(seed: int = 0) -> dict:
    key = jax.random.key(seed)
    ks = jax.random.split(key, 16)
    f32 = jnp.float32
    L = DEPTH
    x = jax.random.normal(ks[0], (BATCH, SEQ, D_MODEL), f32)
    norm_g = 1.0 + 0.02 * jax.random.normal(ks[1], (L, D_MODEL), f32)
    w_in = jax.random.normal(ks[2], (L, D_MODEL, IN_W), f32) * (D_MODEL ** -0.5)
    w_out = jax.random.normal(ks[3], (L, MIX_W, D_MODEL), f32) * (MIX_W ** -0.5)
    b_mu = jax.random.uniform(ks[4], (L, B_SHIFT_W), f32)
    b_w0 = jnp.linspace(-6.0, -1.0, B_W, dtype=f32)[None, :] + 0.1 * jax.random.normal(ks[5], (L, B_W), f32)
    b_w_up = jax.random.normal(ks[6], (L, B_DECAY_RANK, B_W), f32) * (0.1 * B_DECAY_RANK ** -0.5)
    b_a0 = 0.1 * jax.random.normal(ks[7], (L, B_W), f32)
    b_a_up = jax.random.normal(ks[8], (L, B_AAA_RANK, B_W), f32) * (0.5 * B_AAA_RANK ** -0.5)
    b_k_k = 0.85 + 0.05 * jax.random.normal(ks[9], (L, B_W), f32)
    b_k_a = 1.0 + 0.05 * jax.random.normal(ks[10], (L, B_W), f32)
    b_r_k = 0.1 * jax.random.normal(ks[11], (L, B_HEADS, B_HEAD), f32)
    b_ln_g = 1.0 + 0.02 * jax.random.normal(ks[12], (L, B_W), f32)
    b_ln_b = 0.02 * jax.random.normal(ks[13], (L, B_W), f32)
    c_rel_bias = 0.1 * jax.random.normal(ks[14], (L, C_HEADS, C_N_REL), f32)
    final_g = 1.0 + 0.02 * jax.random.normal(ks[15], (D_MODEL,), f32)
    return {"x": x, "norm_g": norm_g, "w_in": w_in, "w_out": w_out, "b_mu": b_mu,
            "b_w0": b_w0, "b_w_up": b_w_up, "b_a0": b_a0, "b_a_up": b_a_up,
            "b_k_k": b_k_k, "b_k_a": b_k_a, "b_r_k": b_r_k, "b_ln_g": b_ln_g,
            "b_ln_b": b_ln_b, "c_rel_bias": c_rel_bias, "final_g": final_g}


def reference(x, norm_g, w_in, w_out, b_mu, b_w0, b_w_up, b_a0, b_a_up, b_k_k, b_k_a,
              b_r_k, b_ln_g, b_ln_b, c_rel_bias, final_g):
    B_, S_ = x.shape[0], x.shape[1]
    pos = jnp.arange(S_)
    for l in range(DEPTH):
        h = rms_norm(x, norm_g[l]) @ w_in[l]
        ha, hb, gb, hc = _split(h, (A_IN, B_SHIFT_W, B_W, C_IN))
        qa, ka, va, ga = _split(ha, (A_QK, A_QK, A_W, A_W))
        qa = rotary(qa.reshape(B_, S_, A_HEADS, A_DK), pos)
        ka = rotary(ka.reshape(B_, S_, A_HEADS, A_DK), pos)
        ya = retention(qa, ka, va.reshape(B_, S_, A_HEADS, A_DV))
        yb = rwkv7_time_mix(hb, b_mu[l], b_w0[l], b_w_up[l], b_a0[l], b_a_up[l],
                            b_k_k[l], b_k_a[l], b_r_k[l], b_ln_g[l], b_ln_b[l])
        qc, kc, vc, gc = _split(hc, (C_W, C_W, C_W, C_W))
        yc = chunk_rel_attention(qc.reshape(B_, S_, C_HEADS, C_DH),
                                 kc.reshape(B_, S_, C_HEADS, C_DH),
                                 vc.reshape(B_, S_, C_HEADS, C_DH), c_rel_bias[l])
        mixed = jnp.concatenate([ya * jax.nn.silu(ga), yb * jax.nn.silu(gb),
                                 yc * jax.nn.silu(gc)], axis=-1)
        x = x + mixed @ w_out[l]
    return rms_norm(x, final_g)
```

```python
import functools
import math

import numpy as np
import jax
import jax.numpy as jnp
from jax import lax
from jax.experimental import pallas as pl
from jax.experimental.pallas import tpu as pltpu

F32 = jnp.float32
BF16 = jnp.bfloat16

D_MODEL = 4096
DEPTH = 4
CHUNK = 64
A_W = 1024
A_HEADS = 4
A_DV = 256
A_DK = 128
A_QK = 512
A_IN = 2 * A_QK + 2 * A_W
ROPE_BASE = 10000.0
A_NORM_EPS = 1e-6
B_W = 1024
B_HEAD = 64
B_HEADS = 16
B_RANK = 128
B_SHIFT_W = 3 * B_W + 2 * B_RANK
B_LN_EPS = 64e-5
B_IN = B_SHIFT_W + B_W
C_W = 2048
C_HEADS = 16
C_DH = 128
C_BACK = 8
C_REL_CLIP = 128
RMS_EPS = 1e-6

LANES = 128
VMEM_LIMIT = 56 * 1024 * 1024

NEG = -1e30


def _params(sem):
    return pltpu.CompilerParams(dimension_semantics=sem, vmem_limit_bytes=VMEM_LIMIT)


def _dot(a, b, prec=None):
    return jnp.dot(a, b, preferred_element_type=F32, precision=prec)


def _dot_nt(a, b, prec=None):
    return lax.dot_general(a, b, (((1,), (1,)), ((), ())), preferred_element_type=F32,
                           precision=prec)


def _dot_tn(a, b, prec=None):
    return lax.dot_general(a, b, (((0,), (0,)), ((), ())), preferred_element_type=F32,
                           precision=prec)


def _silu(g):
    return g * jax.nn.sigmoid(g)


def _rmsnorm_kernel(x_ref, g_ref, o_ref):
    x = x_ref[...]
    ms = jnp.mean(x * x, axis=-1, keepdims=True)
    o_ref[...] = (x * lax.rsqrt(ms + RMS_EPS) * g_ref[...]).astype(o_ref.dtype)


def rmsnorm(x, g, out_dtype, tm=256):
    s, d = x.shape
    return pl.pallas_call(
        _rmsnorm_kernel,
        grid=(s // tm,),
        in_specs=[pl.BlockSpec((tm, d), lambda i: (i, 0)),
                  pl.BlockSpec((1, d), lambda i: (0, 0))],
        out_specs=pl.BlockSpec((tm, d), lambda i: (i, 0)),
        out_shape=jax.ShapeDtypeStruct((s, d), out_dtype),
        compiler_params=_params(("parallel",)),
        name="rmsnorm",
    )(x, g.reshape(1, d))


def _matmul_kernel(a_ref, b_ref, o_ref):
    o_ref[...] = _dot(a_ref[...], b_ref[...]).astype(o_ref.dtype)


def matmul(a, b, out_dtype, tm, tn, name):
    m, k = a.shape
    n = b.shape[1]
    return pl.pallas_call(
        _matmul_kernel,
        grid=(m // tm, n // tn),
        in_specs=[pl.BlockSpec((tm, k), lambda i, j: (i, 0)),
                  pl.BlockSpec((k, tn), lambda i, j: (0, j))],
        out_specs=pl.BlockSpec((tm, tn), lambda i, j: (i, j)),
        out_shape=jax.ShapeDtypeStruct((m, n), out_dtype),
        compiler_params=_params(("parallel", "parallel")),
        name=name,
    )(a, b)


def _outproj_kernel(ya_ref, yb_ref, yc_ref, wa_ref, wb_ref, wc_ref, x_ref, o_ref):
    acc = _dot(ya_ref[...], wa_ref[...])
    acc += _dot(yb_ref[...], wb_ref[...])
    acc += _dot(yc_ref[...], wc_ref[...])
    o_ref[...] = x_ref[...] + acc


def outproj(ya, yb, yc, w, x, tm=512, tn=1024):
    s = x.shape[0]
    n = w.shape[1]
    return pl.pallas_call(
        _outproj_kernel,
        grid=(s // tm, n // tn),
        in_specs=[pl.BlockSpec((tm, A_W), lambda i, j: (i, 0)),
                  pl.BlockSpec((tm, B_W), lambda i, j: (i, 0)),
                  pl.BlockSpec((tm, C_W), lambda i, j: (i, 0)),
                  pl.BlockSpec((A_W, tn), lambda i, j: (0, j)),
                  pl.BlockSpec((B_W, tn), lambda i, j: (A_W // B_W, j)),
                  pl.BlockSpec((C_W, tn), lambda i, j: ((A_W + B_W) // C_W, j)),
                  pl.BlockSpec((tm, tn), lambda i, j: (i, j))],
        out_specs=pl.BlockSpec((tm, tn), lambda i, j: (i, j)),
        out_shape=jax.ShapeDtypeStruct((s, n), F32),
        compiler_params=_params(("parallel", "parallel")),
        name="outproj",
    )(ya, yb, yc, w, w, w, x)


C_TQ = 256
C_NKB = C_BACK * CHUNK // C_TQ + 1
C_TK = C_NKB * C_TQ


def _attn_kernel(q_ref, k0_ref, k1_ref, k2_ref, v0_ref, v1_ref, v2_ref, g_ref, bias_ref,
                 o_ref):
    i = pl.program_id(1)
    q = (q_ref[...] * (C_DH ** -0.5)).astype(BF16)
    k = jnp.concatenate([k0_ref[...], k1_ref[...], k2_ref[...]], axis=0).astype(BF16)
    v = jnp.concatenate([v0_ref[...], v1_ref[...], v2_ref[...]], axis=0).astype(BF16)
    s = _dot_nt(q, k) + bias_ref[0]
    kpos = lax.broadcasted_iota(jnp.int32, (1, C_TK), 1) + (i - (C_NKB - 1)) * C_TQ
    s = jnp.where(kpos >= 0, s, NEG)
    m = jnp.max(s, axis=-1, keepdims=True)
    p = jnp.exp(s - m)
    l = jnp.sum(p, axis=-1, keepdims=True)
    o = _dot(p.astype(BF16), v) / l
    o_ref[...] = (o * _silu(g_ref[...].astype(F32))).astype(o_ref.dtype)


def _attn_bias_tables(rel_bias):
    qi = np.arange(C_TQ)[:, None]
    ki = np.arange(C_TK)[None, :] - (C_NKB - 1) * C_TQ
    dist = qi - ki
    dchunk = qi // CHUNK - np.floor_divide(ki, CHUNK)
    valid = (dchunk >= 0) & (dchunk <= C_BACK)
    idx = np.clip(dist, -(CHUNK - 1), C_REL_CLIP) + (CHUNK - 1)
    tbl = jnp.take(rel_bias, jnp.asarray(idx.reshape(-1), jnp.int32), axis=-1)
    tbl = tbl.reshape(rel_bias.shape[:-1] + (C_TQ, C_TK))
    return jnp.where(jnp.asarray(valid), tbl, NEG).astype(F32)


def chunk_attention(hc, bias_tbl, out_dtype):
    s = hc.shape[0]
    nq = s // C_TQ
    hq = C_W // C_DH

    def kv_spec(base, back):
        return pl.BlockSpec((C_TQ, C_DH),
                            lambda h, i: (jnp.maximum(i - back, 0), base + h))

    return pl.pallas_call(
        _attn_kernel,
        grid=(C_HEADS, nq),
        in_specs=[pl.BlockSpec((C_TQ, C_DH), lambda h, i: (i, h)),
                  kv_spec(hq, 2), kv_spec(hq, 1), kv_spec(hq, 0),
                  kv_spec(2 * hq, 2), kv_spec(2 * hq, 1), kv_spec(2 * hq, 0),
                  pl.BlockSpec((C_TQ, C_DH), lambda h, i: (i, 3 * hq + h)),
                  pl.BlockSpec((1, C_TQ, C_TK), lambda h, i: (h, 0, 0))],
        out_specs=pl.BlockSpec((C_TQ, C_DH), lambda h, i: (i, h)),
        out_shape=jax.ShapeDtypeStruct((s, C_W), out_dtype),
        compiler_params=_params(("parallel", "parallel")),
        name="chunk_attn",
    )(hc, hc, hc, hc, hc, hc, hc, hc, bias_tbl)


A_TB = 256


def _swap_pairs(x):
    lane = lax.broadcasted_iota(jnp.int32, x.shape, x.ndim - 1)
    nxt = pltpu.roll(x, LANES - 1, x.ndim - 1)
    prv = pltpu.roll(x, 1, x.ndim - 1)
    return jnp.where(lane % 2 == 0, nxt, prv)


def _retention_kernel(q_ref, k_ref, v_ref, g_ref, cos_ref, sin_ref, mask_ref, dq_ref,
                      dk_ref, gb_ref, o_ref, st_ref):
    i = pl.program_id(1)

    @pl.when(i == 0)
    def _():
        st_ref[...] = jnp.zeros_like(st_ref)

    cos = cos_ref[...]
    sin = sin_ref[...]
    q = q_ref[...].astype(F32)
    k = k_ref[...].astype(F32)
    q = q * cos + _swap_pairs(q) * sin
    k = (k * cos + _swap_pairs(k) * sin) * (A_DK ** -0.5)
    v = v_ref[...].astype(BF16)
    st = st_ref[...]
    sc = _dot_nt(q.astype(BF16), k.astype(BF16)) * mask_ref[0]
    o = _dot(sc.astype(BF16), v) + _dot((q * dq_ref[0]).astype(BF16), st.astype(BF16))
    st_ref[...] = gb_ref[0] * st + _dot_tn((k * dk_ref[0]).astype(BF16), v)
    o = o * lax.rsqrt(jnp.mean(o * o, axis=-1, keepdims=True) + A_NORM_EPS)
    o_ref[...] = (o * _silu(g_ref[...].astype(F32))).astype(o_ref.dtype)


def _retention_tables():
    h = np.arange(A_HEADS, dtype=np.float64)
    log_g = np.log(1.0 - 2.0 ** (-5.0 - h))
    pos = np.arange(A_TB)
    d = pos[:, None] - pos[None, :]
    cq = pos[:, None] // CHUNK
    ck = pos[None, :] // CHUNK
    same = cq == ck
    past = ck < cq
    expo = np.where(same, np.abs(d), np.where(past, d, 0)).astype(np.float64)
    mask = np.exp(log_g[:, None, None] * expo[None]) * (same | past)[None]
    dq = np.exp(log_g[:, None] * (pos + 1.0)[None, :])
    dk = np.exp(log_g[:, None] * (A_TB - 1.0 - pos)[None, :])
    gb = np.exp(log_g * A_TB)
    ones = np.ones((1, 1, LANES))
    return (jnp.asarray(mask, F32), jnp.asarray(dq[:, :, None] * ones, F32),
            jnp.asarray(dk[:, :, None] * ones, F32),
            jnp.asarray(gb[:, None, None] * np.ones((1, 1, A_DV)), F32))


def _rotary_tables(s):
    inv = 1.0 / (ROPE_BASE ** (jnp.arange(0, A_DK, 2, dtype=F32) / A_DK))
    ang = jnp.arange(s).astype(F32)[:, None] * inv[None, :]
    cos = jnp.repeat(jnp.cos(ang), 2, axis=-1)
    sin = jnp.repeat(jnp.sin(ang), 2, axis=-1)
    sign = jnp.asarray(np.tile(np.array([-1.0, 1.0]), A_DK // 2), F32)
    return cos, sin * sign[None, :]


def retention(ha, rot_tbl, ret_tbl, out_dtype):
    s = ha.shape[0]
    cos, sin = rot_tbl
    mask, dq, dk, gb = ret_tbl
    kq = A_QK // A_DK
    vo = 2 * A_QK // A_DV
    go = vo + A_W // A_DV
    return pl.pallas_call(
        _retention_kernel,
        grid=(A_HEADS, s // A_TB),
        in_specs=[pl.BlockSpec((A_TB, A_DK), lambda h, i: (i, h)),
                  pl.BlockSpec((A_TB, A_DK), lambda h, i: (i, kq + h)),
                  pl.BlockSpec((A_TB, A_DV), lambda h, i: (i, vo + h)),
                  pl.BlockSpec((A_TB, A_DV), lambda h, i: (i, go + h)),
                  pl.BlockSpec((A_TB, A_DK), lambda h, i: (i, 0)),
                  pl.BlockSpec((A_TB, A_DK), lambda h, i: (i, 0)),
                  pl.BlockSpec((1, A_TB, A_TB), lambda h, i: (h, 0, 0)),
                  pl.BlockSpec((1, A_TB, LANES), lambda h, i: (h, 0, 0)),
                  pl.BlockSpec((1, A_TB, LANES), lambda h, i: (h, 0, 0)),
                  pl.BlockSpec((1, 1, A_DV), lambda h, i: (h, 0, 0))],
        out_specs=pl.BlockSpec((A_TB, A_DV), lambda h, i: (i, h)),
        out_shape=jax.ShapeDtypeStruct((s, A_W), out_dtype),
        scratch_shapes=[pltpu.VMEM((A_DK, A_DV), F32)],
        compiler_params=_params(("parallel", "arbitrary")),
        name="retention",
    )(ha, ha, ha, ha, cos, sin, mask, dq, dk, gb)


B_TP = 512
B_TC = 512
B_PREC = lax.Precision.HIGHEST


def _seg_ones():
    r = lax.broadcasted_iota(jnp.int32, (LANES, LANES), 0) // B_HEAD
    c = lax.broadcasted_iota(jnp.int32, (LANES, LANES), 1) // B_HEAD
    return (r == c).astype(F32)


def _seg_sum(x):
    ones = _seg_ones()
    parts = [_dot(x[:, j:j + LANES], ones, B_PREC) for j in range(0, x.shape[1], LANES)]
    return jnp.concatenate(parts, axis=1)


def _rwkv_prep_kernel(hb_ref, prev_ref, mu_ref, w0_ref, wup_ref, a0_ref, aup_ref, kk_ref,
                      ka_ref, rk_ref, rt_ref, kq_ref, bt_ref, kt_ref, v_ref, bonus_ref,
                      pl_ref):
    i = pl.program_id(0)
    hb = hb_ref[...]
    rows = lax.broadcasted_iota(jnp.int32, hb.shape, 0)
    prev_row = jnp.where(i == 0, 0.0, prev_ref[7:8, :])
    shifted = jnp.where(rows == 0, prev_row, pltpu.roll(hb, 1, 0))
    xs = hb + (shifted - hb) * mu_ref[...]
    r = xs[:, 0:B_W]
    k = xs[:, B_W:2 * B_W]
    v = xs[:, 2 * B_W:3 * B_W]
    wd = xs[:, 3 * B_W:3 * B_W + B_RANK]
    ad = xs[:, 3 * B_W + B_RANK:3 * B_W + 2 * B_RANK]
    z = w0_ref[...] + _dot(jnp.tanh(wd), wup_ref[...], B_PREC)
    lw = -math.exp(-0.5) * jax.nn.sigmoid(z)
    a = jax.nn.sigmoid(a0_ref[...] + _dot(ad, aup_ref[...], B_PREC))
    kk = k * kk_ref[...]
    nrm = jnp.maximum(jnp.sqrt(_seg_sum(kk * kk)), 1e-12)
    kk = kk / nrm
    kmod = k * (1.0 + (a - 1.0) * ka_ref[...])
    bonus_ref[...] = _seg_sum(r * kmod * rk_ref[...]) * v
    tr = lax.broadcasted_iota(jnp.int32, (B_TP, B_TP), 0)
    tc = lax.broadcasted_iota(jnp.int32, (B_TP, B_TP), 1)
    tri = ((tr >= tc) & (tr // CHUNK == tc // CHUNK)).astype(F32)
    cum = _dot(tri, lw, B_PREC)
    p_inc = jnp.exp(cum)
    p_inv = jnp.exp(-cum)
    rt_ref[...] = r * p_inc
    kq_ref[...] = kk * jnp.exp(cum - lw)
    bt_ref[...] = kk * a * p_inv
    kt_ref[...] = kmod * p_inv
    v_ref[...] = v
    nc = B_TP // CHUNK
    er = lax.broadcasted_iota(jnp.int32, (nc, B_TP), 0)
    ec = lax.broadcasted_iota(jnp.int32, (nc, B_TP), 1)
    pl_ref[...] = jnp.exp(_dot((ec // CHUNK == er).astype(F32), lw, B_PREC))


def rwkv_prep(hbg, mu, w0, w_up, a0, a_up, k_k, k_a, r_k):
    s = hbg.shape[0]
    nb = s // B_TP
    vec = lambda n: pl.BlockSpec((1, n), lambda i: (0, 0))
    full = lambda a, b: pl.BlockSpec((a, b), lambda i: (0, 0))
    big = pl.BlockSpec((B_TP, B_W), lambda i: (i, 0))
    outs = [jax.ShapeDtypeStruct((s, B_W), F32)] * 6 + [
        jax.ShapeDtypeStruct((s // CHUNK, B_W), F32)]
    return pl.pallas_call(
        _rwkv_prep_kernel,
        grid=(nb,),
        in_specs=[pl.BlockSpec((B_TP, B_SHIFT_W), lambda i: (i, 0)),
                  pl.BlockSpec((8, B_SHIFT_W),
                               lambda i: (jnp.maximum(i * (B_TP // 8) - 1, 0), 0)),
                  vec(B_SHIFT_W), vec(B_W), full(B_RANK, B_W), vec(B_W), full(B_RANK, B_W),
                  vec(B_W), vec(B_W), vec(B_W)],
        out_specs=[big] * 6 + [pl.BlockSpec((B_TP // CHUNK, B_W), lambda i: (i, 0))],
        out_shape=outs,
        compiler_params=_params(("parallel",)),
        name="rwkv_prep",
    )(hbg, hbg, mu.reshape(1, -1), w0.reshape(1, -1), w_up, a0.reshape(1, -1), a_up,
      k_k.reshape(1, -1), k_a.reshape(1, -1), r_k.reshape(1, -1))


def _stack_heads(x):
    lane = lax.broadcasted_iota(jnp.int32, x.shape, 1)
    return jnp.concatenate([jnp.where(lane < B_HEAD, x, 0.0),
                            jnp.where(lane >= B_HEAD, x, 0.0)], axis=0)


def _rwkv_scan_kernel(rt_ref, kq_ref, bt_ref, kt_ref, v_ref, bonus_ref, g_ref, pl_ref,
                      lng_ref, lnb_ref, o_ref, zt_ref):
    i = pl.program_id(1)

    @pl.when(i == 0)
    def _():
        zt_ref[...] = jnp.zeros_like(zt_ref)

    n2 = 2 * CHUNK
    r_i = lax.broadcasted_iota(jnp.int32, (n2, n2), 0)
    c_i = lax.broadcasted_iota(jnp.int32, (n2, n2), 1)
    same_head = (r_i // CHUNK) == (c_i // CHUNK)
    strict = (same_head & (r_i > c_i)).astype(F32)
    incl = (same_head & (r_i >= c_i)).astype(F32)
    eye = (r_i == c_i).astype(F32)
    bd16 = ((r_i // 16) == (c_i // 16)).astype(F32)
    bd32 = ((r_i // 32) == (c_i // 32)).astype(F32)
    seg = _seg_ones()
    mm = functools.partial(_dot, prec=B_PREC)

    def chunk(c, carry):
        rows = pl.ds(pl.multiple_of(c * CHUNK, CHUNK), CHUNK)
        rt = _stack_heads(rt_ref[rows, :])
        kq = _stack_heads(kq_ref[rows, :])
        bt = _stack_heads(bt_ref[rows, :])
        kt = _stack_heads(kt_ref[rows, :])
        vv = _stack_heads(v_ref[rows, :])
        ab = _dot_nt(jnp.concatenate([kq, rt], axis=0), jnp.concatenate([bt, kt], axis=0),
                     B_PREC)
        a_m = ab[:n2, :n2] * strict
        b_m = ab[:n2, n2:] * strict
        aq = ab[n2:, :n2] * incl
        bq = ab[n2:, n2:] * incl
        nn = -(a_m * bd16)
        t = eye + nn
        sq = mm(nn, nn)
        t = t + mm(t, sq)
        sq = mm(sq, sq)
        t = t + mm(t, sq)
        sq = mm(sq, sq)
        t = t + mm(t, sq)
        t = t - mm(t, mm(a_m * (bd32 - bd16), t))
        t = t - mm(t, mm(a_m * (1.0 - bd32), t))
        tw = mm(t, jnp.concatenate([kq, mm(b_m, vv)], axis=1))
        w = tw[:, :n2]
        u0 = -tw[:, n2:]
        zt = zt_ref[...]
        wr = _dot_nt(jnp.concatenate([w, rt], axis=0), zt, B_PREC)
        u = u0 - wr[:n2]
        y2 = wr[n2:] + mm(aq, u) + mm(bq, vv)
        znew = zt + _dot_tn(jnp.concatenate([u, vv], axis=0),
                            jnp.concatenate([bt, kt], axis=0), B_PREC)
        zt_ref[...] = znew * pl_ref[pl.ds(c, 1), :]
        y = y2[:CHUNK] + y2[CHUNK:]
        mean = _dot(y, seg, B_PREC) * (1.0 / B_HEAD)
        yc = y - mean
        var = _dot(yc * yc, seg, B_PREC) * (1.0 / B_HEAD)
        yn = yc * lax.rsqrt(var + B_LN_EPS) * lng_ref[...] + lnb_ref[...]
        out = (yn + bonus_ref[rows, :]) * _silu(g_ref[rows, :].astype(F32))
        o_ref[rows, :] = out.astype(o_ref.dtype)
        return carry

    lax.fori_loop(0, B_TC // CHUNK, chunk, 0)


def rwkv_scan(prep, hbg, ln_g, ln_b, out_dtype):
    rt, kq, bt, kt, v, bonus, p_end = prep
    s = rt.shape[0]
    npair = B_W // LANES
    blk = pl.BlockSpec((B_TC, LANES), lambda p, i: (i, p))
    gate = pl.BlockSpec((B_TC, LANES), lambda p, i: (i, B_SHIFT_W // LANES + p))
    vec = pl.BlockSpec((1, LANES), lambda p, i: (0, p))
    return pl.pallas_call(
        _rwkv_scan_kernel,
        grid=(npair, s // B_TC),
        in_specs=[blk] * 6 + [gate,
                              pl.BlockSpec((B_TC // CHUNK, LANES), lambda p, i: (i, p)),
                              vec, vec],
        out_specs=blk,
        out_shape=jax.ShapeDtypeStruct((s, B_W), out_dtype),
        scratch_shapes=[pltpu.VMEM((LANES, LANES), F32)],
        compiler_params=_params(("parallel", "arbitrary")),
        name="rwkv_scan",
    )(rt, kq, bt, kt, v, bonus, hbg, p_end, ln_g.reshape(1, -1), ln_b.reshape(1, -1))


def kernel(x, norm_g, w_in, w_out, b_mu, b_w0, b_w_up, b_a0, b_a_up, b_k_k, b_k_a, b_r_k,
           b_ln_g, b_ln_b, c_rel_bias, final_g):
    bsz, s, d = x.shape
    assert bsz == 1 and d == D_MODEL
    xs = x.reshape(s, d)
    rot_tbl = _rotary_tables(s)
    ret_tbl = _retention_tables()
    bias_tbl = _attn_bias_tables(c_rel_bias)
    b0 = A_IN
    c0 = A_IN + B_IN
    for l in range(DEPTH):
        wl = w_in[l]
        wa = wl[:, :b0].astype(BF16)
        wb = wl[:, b0:c0].astype(BF16)
        wc = wl[:, c0:].astype(BF16)
        wo = w_out[l].astype(BF16)
        xn = rmsnorm(xs, norm_g[l], BF16)
        ha = matmul(xn, wa, F32, 512, 1024, "proj_a")
        hbg = matmul(xn, wb, F32, 1024, 256, "proj_b")
        hc = matmul(xn, wc, F32, 512, 1024, "proj_c")
        ya = retention(ha, rot_tbl, ret_tbl, BF16)
        prep = rwkv_prep(hbg, b_mu[l], b_w0[l], b_w_up[l], b_a0[l], b_a_up[l], b_k_k[l],
                         b_k_a[l], b_r_k[l].reshape(-1))
        yb = rwkv_scan(prep, hbg, b_ln_g[l], b_ln_b[l], BF16)
        yc = chunk_attention(hc, bias_tbl[l], BF16)
        xs = outproj(ya, yb, yc, wo, xs)
    return rmsnorm(xs, final_g, F32).reshape(bsz, s, d)
```

```python
import math

import numpy as np
import jax
import jax.numpy as jnp
from jax import lax
from jax.experimental import pallas as pl
from jax.experimental.pallas import tpu as pltpu

F32 = jnp.float32
BF16 = jnp.bfloat16

D_MODEL = 4096
DEPTH = 4
CHUNK = 64
A_W = 1024
A_HEADS = 4
A_DV = 256
A_DK = 128
A_QK = 512
A_IN = 2 * A_QK + 2 * A_W
ROPE_BASE = 10000.0
A_NORM_EPS = 1e-6
B_W = 1024
B_HEAD = 64
B_HEADS = 16
B_RANK = 128
B_SHIFT_W = 3 * B_W + 2 * B_RANK
B_LN_EPS = 64e-5
B_IN = B_SHIFT_W + B_W
C_W = 2048
C_HEADS = 16
C_DH = 128
C_BACK = 8
C_REL_CLIP = 128
RMS_EPS = 1e-6

LANES = 128
VMEM_LIMIT = 56 * 1024 * 1024

NEG = -1e30


def _params(sem):
    return pltpu.CompilerParams(dimension_semantics=sem, vmem_limit_bytes=VMEM_LIMIT)


def _dot(a, b, prec=None):
    return jnp.dot(a, b, preferred_element_type=F32, precision=prec)


def _dot_nt(a, b, prec=None):
    return lax.dot_general(a, b, (((1,), (1,)), ((), ())), preferred_element_type=F32,
                           precision=prec)


def _dot_tn(a, b, prec=None):
    return lax.dot_general(a, b, (((0,), (0,)), ((), ())), preferred_element_type=F32,
                           precision=prec)


def _silu(g):
    return g * jax.nn.sigmoid(g)


def _rmsnorm_kernel(x_ref, g_ref, o_ref):
    x = x_ref[...]
    ms = jnp.mean(x * x, axis=-1, keepdims=True)
    o_ref[...] = (x * lax.rsqrt(ms + RMS_EPS) * g_ref[...]).astype(o_ref.dtype)


def rmsnorm(x, g, out_dtype, tm=256):
    s, d = x.shape
    return pl.pallas_call(
        _rmsnorm_kernel,
        grid=(s // tm,),
        in_specs=[pl.BlockSpec((tm, d), lambda i: (i, 0)),
                  pl.BlockSpec((1, d), lambda i: (0, 0))],
        out_specs=pl.BlockSpec((tm, d), lambda i: (i, 0)),
        out_shape=jax.ShapeDtypeStruct((s, d), out_dtype),
        compiler_params=_params(("parallel",)),
        name="rmsnorm",
    )(x, g.reshape(1, d))


def _matmul_kernel(a_ref, b_ref, o_ref):
    o_ref[...] = _dot(a_ref[...], b_ref[...]).astype(o_ref.dtype)


def matmul(a, b, out_dtype, tm, tn, name):
    m, k = a.shape
    n = b.shape[1]
    return pl.pallas_call(
        _matmul_kernel,
        grid=(m // tm, n // tn),
        in_specs=[pl.BlockSpec((tm, k), lambda i, j: (i, 0)),
                  pl.BlockSpec((k, tn), lambda i, j: (0, j))],
        out_specs=pl.BlockSpec((tm, tn), lambda i, j: (i, j)),
        out_shape=jax.ShapeDtypeStruct((m, n), out_dtype),
        compiler_params=_params(("parallel", "parallel")),
        name=name,
    )(a, b)


def _outproj_kernel(ya_ref, yb_ref, yc_ref, wa_ref, wb_ref, wc_ref, x_ref, o_ref):
    acc = _dot(ya_ref[...], wa_ref[...])
    acc += _dot(yb_ref[...], wb_ref[...])
    acc += _dot(yc_ref[...], wc_ref[...])
    o_ref[...] = x_ref[...] + acc


def outproj(ya, yb, yc, w, x, tm=512, tn=1024):
    s = x.shape[0]
    n = w.shape[1]
    return pl.pallas_call(
        _outproj_kernel,
        grid=(s // tm, n // tn),
        in_specs=[pl.BlockSpec((tm, A_W), lambda i, j: (i, 0)),
                  pl.BlockSpec((tm, B_W), lambda i, j: (i, 0)),
                  pl.BlockSpec((tm, C_W), lambda i, j: (i, 0)),
                  pl.BlockSpec((A_W, tn), lambda i, j: (0, j)),
                  pl.BlockSpec((B_W, tn), lambda i, j: (A_W // B_W, j)),
                  pl.BlockSpec((C_W, tn), lambda i, j: ((A_W + B_W) // C_W, j)),
                  pl.BlockSpec((tm, tn), lambda i, j: (i, j))],
        out_specs=pl.BlockSpec((tm, tn), lambda i, j: (i, j)),
        out_shape=jax.ShapeDtypeStruct((s, n), F32),
        compiler_params=_params(("parallel", "parallel")),
        name="outproj",
    )(ya, yb, yc, w, w, w, x)


C_TQ = 256
C_NKB = C_BACK * CHUNK // C_TQ + 1
C_TK = C_NKB * C_TQ
C_TROW = C_TQ + C_TK


C_HPS = 4


def _attn_kernel(q_ref, k0_ref, k1_ref, k2_ref, v0_ref, v1_ref, v2_ref, g_ref, bias_ref,
                 o_ref):
    i = pl.program_id(1)
    kpos = lax.broadcasted_iota(jnp.int32, (1, C_TK), 1) + (i - (C_NKB - 1)) * C_TQ
    heads = [slice(h * C_DH, (h + 1) * C_DH) for h in range(C_HPS)]
    s = [_dot_nt(q_ref[:, c],
                 jnp.concatenate([k0_ref[:, c], k1_ref[:, c], k2_ref[:, c]], axis=0))
         for c in heads]
    s = [jnp.where(kpos >= 0, x * (C_DH ** -0.5) + bias_ref[h], NEG)
         for h, x in enumerate(s)]
    p = [jnp.exp(x - jnp.max(x, axis=-1, keepdims=True)) for x in s]
    l = [jnp.sum(x, axis=-1, keepdims=True) for x in p]
    o = [_dot(x.astype(BF16),
              jnp.concatenate([v0_ref[:, c], v1_ref[:, c], v2_ref[:, c]], axis=0))
         for x, c in zip(p, heads)]
    for h, c in enumerate(heads):
        o_ref[:, c] = (o[h] / l[h] * _silu(g_ref[:, c].astype(F32))).astype(o_ref.dtype)


def _bias_kernel(r_ref, o_ref):
    x = jnp.broadcast_to(r_ref[0], (C_TQ, C_TROW))
    y = pltpu.roll(x, 0, 1, stride=1, stride_axis=0)[:, :C_TK]
    qc = lax.broadcasted_iota(jnp.int32, (C_TQ, C_TK), 0) // CHUNK
    kc = lax.broadcasted_iota(jnp.int32, (C_TQ, C_TK), 1) // CHUNK - (C_TK - C_TQ) // CHUNK
    dchunk = qc - kc
    o_ref[0] = jnp.where((dchunk >= 0) & (dchunk <= C_BACK), y, NEG)


def _attn_bias_tables(rel_bias):
    j = np.arange(C_TROW)
    dist = (C_TK - j) % C_TROW - C_TQ
    idx = np.clip(dist, -(CHUNK - 1), C_REL_CLIP) + (CHUNK - 1)
    n = rel_bias.shape[0] * rel_bias.shape[1]
    row = jnp.take(rel_bias, jnp.asarray(idx, jnp.int32), axis=-1).reshape(n, 1, C_TROW)
    return pl.pallas_call(
        _bias_kernel,
        grid=(n,),
        in_specs=[pl.BlockSpec((1, 1, C_TROW), lambda i: (i, 0, 0))],
        out_specs=pl.BlockSpec((1, C_TQ, C_TK), lambda i: (i, 0, 0)),
        out_shape=jax.ShapeDtypeStruct((n, C_TQ, C_TK), F32),
        compiler_params=_params(("parallel",)),
        name="attn_bias",
    )(row)


def chunk_attention(qkv, gate, bias_tbl, layer, out_dtype):
    s = qkv.shape[0]
    nq = s // C_TQ
    wblk = C_HPS * C_DH
    ng = C_HEADS // C_HPS

    def kv_spec(base, back):
        return pl.BlockSpec((C_TQ, wblk),
                            lambda h, i: (jnp.maximum(i - back, 0), base + h))

    return pl.pallas_call(
        _attn_kernel,
        grid=(ng, nq),
        in_specs=[pl.BlockSpec((C_TQ, wblk), lambda h, i: (i, h)),
                  kv_spec(ng, 2), kv_spec(ng, 1), kv_spec(ng, 0),
                  kv_spec(2 * ng, 2), kv_spec(2 * ng, 1), kv_spec(2 * ng, 0),
                  pl.BlockSpec((C_TQ, wblk), lambda h, i: (i, h)),
                  pl.BlockSpec((C_HPS, C_TQ, C_TK), lambda h, i: (layer * ng + h, 0, 0))],
        out_specs=pl.BlockSpec((C_TQ, wblk), lambda h, i: (i, h)),
        out_shape=jax.ShapeDtypeStruct((s, C_W), out_dtype),
        compiler_params=_params(("parallel", "parallel")),
        name="chunk_attn",
    )(qkv, qkv, qkv, qkv, qkv, qkv, qkv, gate, bias_tbl)


A_TB = 256


def _swap_pairs(x):
    lane = lax.broadcasted_iota(jnp.int32, x.shape, x.ndim - 1)
    nxt = pltpu.roll(x, LANES - 1, x.ndim - 1)
    prv = pltpu.roll(x, 1, x.ndim - 1)
    return jnp.where(lane % 2 == 0, nxt, prv)


def _retention_kernel(q_ref, k_ref, v_ref, g_ref, cos_ref, sin_ref, mask_ref, dq_ref,
                      dk_ref, gb_ref, o_ref, st_ref):
    i = pl.program_id(1)

    @pl.when(i == 0)
    def _():
        st_ref[...] = jnp.zeros_like(st_ref)

    cos = cos_ref[...]
    sin = sin_ref[...]
    q = q_ref[...].astype(F32)
    k = k_ref[...].astype(F32)
    q = q * cos + _swap_pairs(q) * sin
    k = (k * cos + _swap_pairs(k) * sin) * (A_DK ** -0.5)
    v = v_ref[...].astype(BF16)
    st = st_ref[...]
    sc = _dot_nt(q.astype(BF16), k.astype(BF16)) * mask_ref[0]
    o = _dot(sc.astype(BF16), v) + _dot((q * dq_ref[0]).astype(BF16), st.astype(BF16))
    st_ref[...] = gb_ref[0] * st + _dot_tn((k * dk_ref[0]).astype(BF16), v)
    o = o * lax.rsqrt(jnp.mean(o * o, axis=-1, keepdims=True) + A_NORM_EPS)
    o_ref[...] = (o * _silu(g_ref[...].astype(F32))).astype(o_ref.dtype)


def _retention_tables():
    h = np.arange(A_HEADS, dtype=np.float64)
    log_g = np.log(1.0 - 2.0 ** (-5.0 - h))
    pos = np.arange(A_TB)
    d = pos[:, None] - pos[None, :]
    cq = pos[:, None] // CHUNK
    ck = pos[None, :] // CHUNK
    same = cq == ck
    past = ck < cq
    expo = np.where(same, np.abs(d), np.where(past, d, 0)).astype(np.float64)
    mask = np.exp(log_g[:, None, None] * expo[None]) * (same | past)[None]
    dq = np.exp(log_g[:, None] * (pos + 1.0)[None, :])
    dk = np.exp(log_g[:, None] * (A_TB - 1.0 - pos)[None, :])
    gb = np.exp(log_g * A_TB)
    ones = np.ones((1, 1, LANES))
    return (jnp.asarray(mask, F32), jnp.asarray(dq[:, :, None] * ones, F32),
            jnp.asarray(dk[:, :, None] * ones, F32),
            jnp.asarray(gb[:, None, None] * np.ones((1, 1, A_DV)), F32))


def _rotary_tables(s):
    inv = 1.0 / (ROPE_BASE ** (jnp.arange(0, A_DK, 2, dtype=F32) / A_DK))
    ang = jnp.arange(s).astype(F32)[:, None] * inv[None, :]
    cos = jnp.repeat(jnp.cos(ang), 2, axis=-1)
    sin = jnp.repeat(jnp.sin(ang), 2, axis=-1)
    sign = jnp.asarray(np.tile(np.array([-1.0, 1.0]), A_DK // 2), F32)
    return cos, sin * sign[None, :]


def retention(ha, rot_tbl, ret_tbl, out_dtype):
    s = ha.shape[0]
    cos, sin = rot_tbl
    mask, dq, dk, gb = ret_tbl
    kq = A_QK // A_DK
    vo = 2 * A_QK // A_DV
    go = vo + A_W // A_DV
    return pl.pallas_call(
        _retention_kernel,
        grid=(A_HEADS, s // A_TB),
        in_specs=[pl.BlockSpec((A_TB, A_DK), lambda h, i: (i, h)),
                  pl.BlockSpec((A_TB, A_DK), lambda h, i: (i, kq + h)),
                  pl.BlockSpec((A_TB, A_DV), lambda h, i: (i, vo + h)),
                  pl.BlockSpec((A_TB, A_DV), lambda h, i: (i, go + h)),
                  pl.BlockSpec((A_TB, A_DK), lambda h, i: (i, 0)),
                  pl.BlockSpec((A_TB, A_DK), lambda h, i: (i, 0)),
                  pl.BlockSpec((1, A_TB, A_TB), lambda h, i: (h, 0, 0)),
                  pl.BlockSpec((1, A_TB, LANES), lambda h, i: (h, 0, 0)),
                  pl.BlockSpec((1, A_TB, LANES), lambda h, i: (h, 0, 0)),
                  pl.BlockSpec((1, 1, A_DV), lambda h, i: (h, 0, 0))],
        out_specs=pl.BlockSpec((A_TB, A_DV), lambda h, i: (i, h)),
        out_shape=jax.ShapeDtypeStruct((s, A_W), out_dtype),
        scratch_shapes=[pltpu.VMEM((A_DK, A_DV), F32)],
        compiler_params=_params(("parallel", "arbitrary")),
        name="retention",
    )(ha, ha, ha, ha, cos, sin, mask, dq, dk, gb)


B_TP = 512
B_TC = 512
B_NCB = B_TC // CHUNK
B_NPAIR = B_W // LANES
B_PREC = lax.Precision.HIGHEST


def _seg_ones():
    r = lax.broadcasted_iota(jnp.int32, (LANES, LANES), 0) // B_HEAD
    c = lax.broadcasted_iota(jnp.int32, (LANES, LANES), 1) // B_HEAD
    return (r == c).astype(BF16)


def _seg_sum(x):
    ones = _seg_ones()
    hi = x.astype(BF16)
    lo = (x - hi.astype(F32)).astype(BF16)
    parts = [_dot(hi[:, j:j + LANES], ones) + _dot(lo[:, j:j + LANES], ones)
             for j in range(0, x.shape[1], LANES)]
    return jnp.concatenate(parts, axis=1)


def _rwkv_prep_kernel(hb_ref, prev_ref, mu_ref, w0_ref, wup_ref, a0_ref, aup_ref, kk_ref,
                      ka_ref, rk_ref, rt_ref, kq_ref, bt_ref, kt_ref, v_ref, bte_ref,
                      kte_ref, bonus_ref, pl_ref):
    i = pl.program_id(0)
    hb = hb_ref[...]
    rows = lax.broadcasted_iota(jnp.int32, hb.shape, 0)
    prev_row = jnp.where(i == 0, 0.0, prev_ref[7:8, :])
    shifted = jnp.where(rows == 0, prev_row, pltpu.roll(hb, 1, 0))
    xs = hb + (shifted - hb) * mu_ref[...]
    r = xs[:, 0:B_W]
    k = xs[:, B_W:2 * B_W]
    v = xs[:, 2 * B_W:3 * B_W]
    wd = xs[:, 3 * B_W:3 * B_W + B_RANK]
    ad = xs[:, 3 * B_W + B_RANK:3 * B_W + 2 * B_RANK]
    z = w0_ref[...] + _dot(jnp.tanh(wd), wup_ref[...], B_PREC)
    lw = -math.exp(-0.5) * jax.nn.sigmoid(z)
    a = jax.nn.sigmoid(a0_ref[...] + _dot(ad, aup_ref[...], B_PREC))
    kk = k * kk_ref[...]
    nrm = jnp.maximum(jnp.sqrt(_seg_sum(kk * kk)), 1e-12)
    kk = kk / nrm
    kmod = k * (1.0 + (a - 1.0) * ka_ref[...])
    bonus_ref[...] = _seg_sum(r * kmod * rk_ref[...]) * v
    tr = lax.broadcasted_iota(jnp.int32, (B_TP, B_TP), 0)
    tc = lax.broadcasted_iota(jnp.int32, (B_TP, B_TP), 1)
    same = tr // CHUNK == tc // CHUNK
    cum = _dot((same & (tr >= tc)).astype(F32), lw, B_PREC)
    tot = _dot(same.astype(F32), lw, B_PREC)
    p_inv = jnp.exp(-cum)
    p_end = jnp.exp(tot - cum)
    rt_ref[...] = (r * jnp.exp(cum)).astype(BF16)
    kq_ref[...] = (kk * jnp.exp(cum - lw)).astype(BF16)
    bt_ref[...] = (kk * a * p_inv).astype(BF16)
    kt_ref[...] = (kmod * p_inv).astype(BF16)
    v_ref[...] = v.astype(BF16)
    bte_ref[...] = (kk * a * p_end).astype(BF16)
    kte_ref[...] = (kmod * p_end).astype(BF16)
    nc = B_TP // CHUNK
    er = lax.broadcasted_iota(jnp.int32, (nc, B_TP), 0)
    ec = lax.broadcasted_iota(jnp.int32, (nc, B_TP), 1)
    pl_ref[...] = jnp.exp(_dot((ec // CHUNK == er).astype(F32), lw, B_PREC))


def rwkv_prep(hb, mu, w0, w_up, a0, a_up, k_k, k_a, r_k):
    s = hb.shape[0]
    nb = s // B_TP
    vec = lambda n: pl.BlockSpec((1, n), lambda i: (0, 0))
    full = lambda a, b: pl.BlockSpec((a, b), lambda i: (0, 0))
    big = pl.BlockSpec((B_TP, B_W), lambda i: (i, 0))
    outs = ([jax.ShapeDtypeStruct((s, B_W), BF16)] * 7
            + [jax.ShapeDtypeStruct((s, B_W), F32),
               jax.ShapeDtypeStruct((s // CHUNK, B_W), F32)])
    return pl.pallas_call(
        _rwkv_prep_kernel,
        grid=(nb,),
        in_specs=[pl.BlockSpec((B_TP, B_SHIFT_W), lambda i: (i, 0)),
                  pl.BlockSpec((8, B_SHIFT_W),
                               lambda i: (jnp.maximum(i * (B_TP // 8) - 1, 0), 0)),
                  vec(B_SHIFT_W), vec(B_W), full(B_RANK, B_W), vec(B_W), full(B_RANK, B_W),
                  vec(B_W), vec(B_W), vec(B_W)],
        out_specs=[big] * 8 + [pl.BlockSpec((B_TP // CHUNK, B_W), lambda i: (i, 0))],
        out_shape=outs,
        compiler_params=_params(("parallel",)),
        name="rwkv_prep",
    )(hb, hb, mu.reshape(1, -1), w0.reshape(1, -1), w_up, a0.reshape(1, -1), a_up,
      k_k.reshape(1, -1), k_a.reshape(1, -1), r_k.reshape(1, -1))


def _stack_heads(x):
    lane = lax.broadcasted_iota(jnp.int32, x.shape, 1)
    zero = jnp.zeros_like(x)
    return jnp.concatenate([jnp.where(lane < B_HEAD, x, zero),
                            jnp.where(lane >= B_HEAD, x, zero)], axis=0)


def _mm(a, b):
    return _dot(a.astype(BF16), b.astype(BF16))


def _rwkv_intra_kernel(rt_ref, kq_ref, bt_ref, kt_ref, v_ref, bte_ref, kte_ref, pl_ref,
                       mz_ref, gz_ref, rq_ref, yc_ref):
    n2 = 2 * CHUNK
    r_i = lax.broadcasted_iota(jnp.int32, (n2, n2), 0)
    c_i = lax.broadcasted_iota(jnp.int32, (n2, n2), 1)
    same_head = (r_i // CHUNK) == (c_i // CHUNK)
    strict = (same_head & (r_i > c_i)).astype(F32)
    incl = (same_head & (r_i >= c_i)).astype(F32)
    eye = (r_i == c_i).astype(F32)
    bd16 = ((r_i // 16) == (c_i // 16)).astype(F32)
    bd32 = ((r_i // 32) == (c_i // 32)).astype(F32)

    cs = range(B_NCB)

    def load(ref):
        return [_stack_heads(ref[c * CHUNK:(c + 1) * CHUNK, :]) for c in cs]

    def each(fn, *lists):
        return [fn(*args) for args in zip(*lists)]

    rt, kq, bt, kt, vv, bte, kte = (load(r) for r in (rt_ref, kq_ref, bt_ref, kt_ref, v_ref,
                                                      bte_ref, kte_ref))
    ab = each(lambda kq_, rt_, bt_, kt_: _dot_nt(jnp.concatenate([kq_, rt_], axis=0),
                                                 jnp.concatenate([bt_, kt_], axis=0)),
              kq, rt, bt, kt)
    a_m = [x[:n2, :n2] * strict for x in ab]
    b_m = [x[:n2, n2:] * strict for x in ab]
    aq = [x[n2:, :n2] * incl for x in ab]
    bq = [x[n2:, n2:] * incl for x in ab]
    nn = [-(x * bd16) for x in a_m]
    t = [eye + x for x in nn]
    sq = each(_mm, nn, nn)
    for step in range(3):
        t = each(lambda t_, s_: t_ + _mm(t_, s_), t, sq)
        if step < 2:
            sq = each(_mm, sq, sq)
    for blk in (bd32 - bd16, 1.0 - bd32):
        e = each(lambda a_, t_: _mm(a_ * blk, t_), a_m, t)
        t = each(lambda t_, e_: t_ - _mm(t_, e_), t, e)
    bv = each(_mm, b_m, vv)
    tw = each(lambda t_, kq_, bv_: _mm(t_, jnp.concatenate([kq_.astype(F32), bv_], axis=1)),
              t, kq, bv)
    w = [x[:, :n2].astype(BF16) for x in tw]
    u0 = [(-x[:, n2:]).astype(BF16) for x in tw]
    aw = each(lambda aq_, w_, u_: _mm(aq_, jnp.concatenate([w_, u_], axis=1)), aq, w, u0)
    bqv = each(_mm, bq, vv)
    wb = each(_dot_tn, w, bte)
    gz = each(lambda u_, v_, b_, k_: _dot_tn(jnp.concatenate([u_, v_], axis=0),
                                             jnp.concatenate([b_, k_], axis=0)),
              u0, vv, bte, kte)
    for c in cs:
        rq_ref[0, c] = (rt[c].astype(F32) - aw[c][:, :n2]).astype(BF16)
        yc2 = bqv[c] + aw[c][:, n2:]
        yc_ref[c * CHUNK:(c + 1) * CHUNK, :] = yc2[:CHUNK] + yc2[CHUNK:]
        mz_ref[0, c] = (eye * pl_ref[c:c + 1, :] - wb[c]).astype(BF16)
        gz_ref[0, c] = gz[c]


def rwkv_intra(prep):
    rt, kq, bt, kt, v, bte, kte, _, p_end = prep
    s = rt.shape[0]
    nchunk = s // CHUNK
    blk = pl.BlockSpec((B_TC, LANES), lambda p, i: (i, p))
    mat = pl.BlockSpec((1, B_NCB, LANES, LANES), lambda p, i: (p, i, 0, 0))
    mshape = (B_NPAIR, nchunk, LANES, LANES)
    return pl.pallas_call(
        _rwkv_intra_kernel,
        grid=(B_NPAIR, s // B_TC),
        in_specs=[blk] * 7 + [pl.BlockSpec((B_NCB, LANES), lambda p, i: (i, p))],
        out_specs=[mat, mat, mat, blk],
        out_shape=[jax.ShapeDtypeStruct(mshape, BF16), jax.ShapeDtypeStruct(mshape, F32),
                   jax.ShapeDtypeStruct(mshape, BF16), jax.ShapeDtypeStruct((s, B_W), F32)],
        compiler_params=_params(("parallel", "parallel")),
        name="rwkv_intra",
    )(rt, kq, bt, kt, v, bte, kte, p_end)


def _rwkv_serial_kernel(mz_ref, gz_ref, rq_ref, yc_ref, bonus_ref, g_ref, lng_ref, lnb_ref,
                        o_ref, zt_ref, y_ref):
    i = pl.program_id(0)

    @pl.when(i == 0)
    def _():
        zt_ref[...] = jnp.zeros_like(zt_ref)

    def chunk(c, carry):
        rows = pl.ds(pl.multiple_of(c * CHUNK, CHUNK), CHUNK)
        for p in range(B_NPAIR):
            cols = slice(p * LANES, (p + 1) * LANES)
            zt = zt_ref[p].astype(BF16)
            y2 = _dot_nt(rq_ref[p, c], zt)
            zt_ref[p] = _dot(zt, mz_ref[p, c]) + gz_ref[p, c]
            y_ref[rows, cols] = y2[:CHUNK] + y2[CHUNK:] + yc_ref[rows, cols]
        return carry

    lax.fori_loop(0, B_NCB, chunk, 0)
    y = y_ref[...]
    mean = _seg_sum(y) * (1.0 / B_HEAD)
    yc = y - mean
    var = _seg_sum(yc * yc) * (1.0 / B_HEAD)
    yn = yc * lax.rsqrt(var + B_LN_EPS) * lng_ref[...] + lnb_ref[...]
    o_ref[...] = ((yn + bonus_ref[...]) * _silu(g_ref[...].astype(F32))).astype(o_ref.dtype)


def rwkv_serial(intra, bonus, gate, ln_g, ln_b, out_dtype):
    mz, gz, rq, yc = intra
    s = yc.shape[0]
    mat = pl.BlockSpec((B_NPAIR, B_NCB, LANES, LANES), lambda i: (0, i, 0, 0))
    blk = pl.BlockSpec((B_TC, B_W), lambda i: (i, 0))
    vec = pl.BlockSpec((1, B_W), lambda i: (0, 0))
    return pl.pallas_call(
        _rwkv_serial_kernel,
        grid=(s // B_TC,),
        in_specs=[mat, mat, mat, blk, blk, blk, vec, vec],
        out_specs=blk,
        out_shape=jax.ShapeDtypeStruct((s, B_W), out_dtype),
        scratch_shapes=[pltpu.VMEM((B_NPAIR, LANES, LANES), F32),
                        pltpu.VMEM((B_TC, B_W), F32)],
        compiler_params=_params(("arbitrary",)),
        name="rwkv_serial",
    )(mz, gz, rq, yc, bonus, gate, ln_g.reshape(1, -1), ln_b.reshape(1, -1))


def rwkv7(hb, gate, mu, w0, w_up, a0, a_up, k_k, k_a, r_k, ln_g, ln_b, out_dtype):
    prep = rwkv_prep(hb, mu, w0, w_up, a0, a_up, k_k, k_a, r_k)
    intra = rwkv_intra(prep)
    return rwkv_serial(intra, prep[7], gate, ln_g, ln_b, out_dtype)


def kernel(x, norm_g, w_in, w_out, b_mu, b_w0, b_w_up, b_a0, b_a_up, b_k_k, b_k_a, b_r_k,
           b_ln_g, b_ln_b, c_rel_bias, final_g):
    bsz, s, d = x.shape
    assert bsz == 1 and d == D_MODEL
    xs = x.reshape(s, d)
    rot_tbl = _rotary_tables(s)
    ret_tbl = _retention_tables()
    bias_tbl = _attn_bias_tables(c_rel_bias)
    b0 = A_IN
    g0 = A_IN + B_SHIFT_W
    c0 = A_IN + B_IN
    cg = c0 + 3 * C_W
    for l in range(DEPTH):
        wl = w_in[l]
        wa = wl[:, :b0].astype(BF16)
        wb = wl[:, b0:g0].astype(BF16)
        wg = wl[:, g0:c0].astype(BF16)
        wc = wl[:, c0:cg].astype(BF16)
        wcg = wl[:, cg:].astype(BF16)
        wo = w_out[l].astype(BF16)
        xn = rmsnorm(xs, norm_g[l], BF16)
        ha = matmul(xn, wa, F32, 512, 1024, "proj_a")
        hb = matmul(xn, wb, F32, 1024, 256, "proj_b")
        gb = matmul(xn, wg, F32, 512, 1024, "proj_gb")
        qkv = matmul(xn, wc, BF16, 512, 1024, "proj_c")
        gc = matmul(xn, wcg, F32, 512, 1024, "proj_gc")
        ya = retention(ha, rot_tbl, ret_tbl, BF16)
        yb = rwkv7(hb, gb, b_mu[l], b_w0[l], b_w_up[l], b_a0[l], b_a_up[l], b_k_k[l],
                   b_k_a[l], b_r_k[l].reshape(-1), b_ln_g[l], b_ln_b[l], BF16)
        yc = chunk_attention(qkv, gc, bias_tbl, l, BF16)
        xs = outproj(ya, yb, yc, wo, xs)
    return rmsnorm(xs, final_g, F32).reshape(bsz, s, d)
```

```python
import math

import numpy as np
import jax
import jax.numpy as jnp
from jax import lax
from jax.experimental import pallas as pl
from jax.experimental.pallas import tpu as pltpu

F32 = jnp.float32
BF16 = jnp.bfloat16

D_MODEL = 4096
DEPTH = 4
CHUNK = 64
A_W = 1024
A_HEADS = 4
A_DV = 256
A_DK = 128
A_QK = 512
A_IN = 2 * A_QK + 2 * A_W
ROPE_BASE = 10000.0
A_NORM_EPS = 1e-6
B_W = 1024
B_HEAD = 64
B_HEADS = 16
B_RANK = 128
B_SHIFT_W = 3 * B_W + 2 * B_RANK
B_LN_EPS = 64e-5
B_IN = B_SHIFT_W + B_W
C_W = 2048
C_HEADS = 16
C_DH = 128
C_BACK = 8
C_REL_CLIP = 128
RMS_EPS = 1e-6

LANES = 128
VMEM_LIMIT = 56 * 1024 * 1024

NEG = -1e30


def _params(sem):
    return pltpu.CompilerParams(dimension_semantics=sem, vmem_limit_bytes=VMEM_LIMIT)


def _dot(a, b, prec=None):
    return jnp.dot(a, b, preferred_element_type=F32, precision=prec)


def _dot_nt(a, b, prec=None):
    return lax.dot_general(a, b, (((1,), (1,)), ((), ())), preferred_element_type=F32,
                           precision=prec)


def _dot_tn(a, b, prec=None):
    return lax.dot_general(a, b, (((0,), (0,)), ((), ())), preferred_element_type=F32,
                           precision=prec)


def _silu(g):
    return g * jax.nn.sigmoid(g)


def _rmsnorm_kernel(x_ref, g_ref, o_ref):
    x = x_ref[...]
    ms = jnp.mean(x * x, axis=-1, keepdims=True)
    o_ref[...] = (x * lax.rsqrt(ms + RMS_EPS) * g_ref[...]).astype(o_ref.dtype)


def rmsnorm(x, g, out_dtype, tm=256):
    s, d = x.shape
    return pl.pallas_call(
        _rmsnorm_kernel,
        grid=(s // tm,),
        in_specs=[pl.BlockSpec((tm, d), lambda i: (i, 0)),
                  pl.BlockSpec((1, d), lambda i: (0, 0))],
        out_specs=pl.BlockSpec((tm, d), lambda i: (i, 0)),
        out_shape=jax.ShapeDtypeStruct((s, d), out_dtype),
        compiler_params=_params(("parallel",)),
        name="rmsnorm",
    )(x, g.reshape(1, d))


P_TM = 2048
P_TN = 256


def _resident_rows(shape):
    return pl.BlockSpec(shape, lambda i, j: (i, 0), pipeline_mode=pl.Buffered(1))


def _proj_kernel(a_ref, w_ref, o_ref):
    o_ref[...] = _dot(a_ref[...], w_ref[0].astype(BF16)).astype(o_ref.dtype)


def _proj_scaled_kernel(a_ref, w_ref, sc_ref, o_ref):
    acc = _dot(a_ref[...], w_ref[0].astype(BF16))
    o_ref[...] = (acc * sc_ref[...]).astype(o_ref.dtype)


def proj(a, w_all, layer, col0, ncols, out_dtype, name, col_scale=None):
    m, k = a.shape
    assert col0 % P_TN == 0 and ncols % P_TN == 0 and m % P_TM == 0
    off = col0 // P_TN
    in_specs = [_resident_rows((P_TM, k)),
                pl.BlockSpec((1, k, P_TN), lambda i, j: (layer, 0, off + j))]
    args = (a, w_all)
    if col_scale is not None:
        in_specs.append(pl.BlockSpec((1, P_TN), lambda i, j: (0, j)))
        args += (col_scale,)
    return pl.pallas_call(
        _proj_kernel if col_scale is None else _proj_scaled_kernel,
        grid=(m // P_TM, ncols // P_TN),
        in_specs=in_specs,
        out_specs=pl.BlockSpec((P_TM, P_TN), lambda i, j: (i, j)),
        out_shape=jax.ShapeDtypeStruct((m, ncols), out_dtype),
        compiler_params=_params(("parallel", "parallel")),
        name=name,
    )(*args)


def _outproj_kernel(ya_ref, yb_ref, yc_ref, wa_ref, wb_ref, wc_ref, x_ref, o_ref):
    acc = _dot(ya_ref[...], wa_ref[0].astype(BF16))
    acc += _dot(yb_ref[...], wb_ref[0].astype(BF16))
    acc += _dot(yc_ref[...], wc_ref[0].astype(BF16))
    o_ref[...] = x_ref[...] + acc


def outproj(ya, yb, yc, w_all, layer, x):
    s = x.shape[0]
    n = w_all.shape[2]
    tm, tn = P_TM, P_TN
    return pl.pallas_call(
        _outproj_kernel,
        grid=(s // tm, n // tn),
        in_specs=[_resident_rows((tm, A_W)), _resident_rows((tm, B_W)),
                  _resident_rows((tm, C_W)),
                  pl.BlockSpec((1, A_W, tn), lambda i, j: (layer, 0, j)),
                  pl.BlockSpec((1, B_W, tn), lambda i, j: (layer, A_W // B_W, j)),
                  pl.BlockSpec((1, C_W, tn), lambda i, j: (layer, (A_W + B_W) // C_W, j)),
                  pl.BlockSpec((tm, tn), lambda i, j: (i, j))],
        out_specs=pl.BlockSpec((tm, tn), lambda i, j: (i, j)),
        out_shape=jax.ShapeDtypeStruct((s, n), F32),
        compiler_params=_params(("parallel", "parallel")),
        name="outproj",
    )(ya, yb, yc, w_all, w_all, w_all, x)


C_TQ = 256
C_NKB = C_BACK * CHUNK // C_TQ + 1
C_TK = C_NKB * C_TQ
C_TROW = C_TQ + C_TK


C_HPS = 4
LOG2E = math.log2(math.e)
C_QSCALE = C_DH ** -0.5 * LOG2E


def _attn_kernel(q_ref, k0_ref, k1_ref, k2_ref, v0_ref, v1_ref, v2_ref, g_ref, bias_ref,
                 o_ref):
    heads = [slice(h * C_DH, (h + 1) * C_DH) for h in range(C_HPS)]
    s = [_dot_nt(q_ref[:, c],
                 jnp.concatenate([k0_ref[:, c], k1_ref[:, c], k2_ref[:, c]], axis=0))
         + bias_ref[h] for h, c in enumerate(heads)]
    p = [jnp.exp2(x - jnp.max(x, axis=-1, keepdims=True)) for x in s]
    l = [jnp.sum(x, axis=-1, keepdims=True) for x in p]
    o = [_dot(x.astype(BF16),
              jnp.concatenate([v0_ref[:, c], v1_ref[:, c], v2_ref[:, c]], axis=0))
         for x, c in zip(p, heads)]
    for h, c in enumerate(heads):
        o_ref[:, c] = (o[h] / l[h] * _silu(g_ref[:, c].astype(F32))).astype(o_ref.dtype)


def _bias_kernel(r_ref, o_ref):
    x = jnp.broadcast_to(r_ref[0], (C_TQ, C_TROW))
    y = pltpu.roll(x, 0, 1, stride=1, stride_axis=0)[:, :C_TK]
    qc = lax.broadcasted_iota(jnp.int32, (C_TQ, C_TK), 0) // CHUNK
    kcol = lax.broadcasted_iota(jnp.int32, (C_TQ, C_TK), 1)
    dchunk = qc - (kcol // CHUNK - (C_TK - C_TQ) // CHUNK)
    band = (dchunk >= 0) & (dchunk <= C_BACK)
    for v in range(C_NKB):
        o_ref[v, 0] = jnp.where(band & (kcol >= (C_NKB - 1 - v) * C_TQ), y, NEG)


def _attn_bias_tables(rel_bias):
    j = np.arange(C_TROW)
    dist = (C_TK - j) % C_TROW - C_TQ
    idx = np.clip(dist, -(CHUNK - 1), C_REL_CLIP) + (CHUNK - 1)
    nl, nh = rel_bias.shape[0], rel_bias.shape[1]
    row = jnp.take(rel_bias, jnp.asarray(idx, jnp.int32), axis=-1) * LOG2E
    return pl.pallas_call(
        _bias_kernel,
        grid=(nl, nh),
        in_specs=[pl.BlockSpec((1, 1, C_TROW), lambda l, h: (l * nh + h, 0, 0))],
        out_specs=pl.BlockSpec((C_NKB, 1, C_TQ, C_TK), lambda l, h: (l, h, 0, 0)),
        out_shape=jax.ShapeDtypeStruct((nl * C_NKB, nh, C_TQ, C_TK), F32),
        compiler_params=_params(("parallel", "parallel")),
        name="attn_bias",
    )(row.reshape(nl * nh, 1, C_TROW))


def chunk_attention(qkv, gate, bias_tbl, layer, out_dtype):
    s = qkv.shape[0]
    nq = s // C_TQ
    wblk = C_HPS * C_DH
    ng = C_HEADS // C_HPS

    def kv_spec(base, back):
        return pl.BlockSpec((C_TQ, wblk),
                            lambda h, i: (jnp.maximum(i - back, 0), base + h))

    def bias_map(h, i):
        return (layer * C_NKB + jnp.minimum(i, C_NKB - 1), h, 0, 0)

    return pl.pallas_call(
        _attn_kernel,
        grid=(ng, nq),
        in_specs=[pl.BlockSpec((C_TQ, wblk), lambda h, i: (i, h)),
                  kv_spec(ng, 2), kv_spec(ng, 1), kv_spec(ng, 0),
                  kv_spec(2 * ng, 2), kv_spec(2 * ng, 1), kv_spec(2 * ng, 0),
                  pl.BlockSpec((C_TQ, wblk), lambda h, i: (i, h)),
                  pl.BlockSpec((None, C_HPS, C_TQ, C_TK), bias_map)],
        out_specs=pl.BlockSpec((C_TQ, wblk), lambda h, i: (i, h)),
        out_shape=jax.ShapeDtypeStruct((s, C_W), out_dtype),
        compiler_params=_params(("parallel", "parallel")),
        name="chunk_attn",
    )(qkv, qkv, qkv, qkv, qkv, qkv, qkv, gate, bias_tbl)


A_TB = 256


def _swap_pairs(x):
    lane = lax.broadcasted_iota(jnp.int32, x.shape, x.ndim - 1)
    nxt = pltpu.roll(x, LANES - 1, x.ndim - 1)
    prv = pltpu.roll(x, 1, x.ndim - 1)
    return jnp.where(lane % 2 == 0, nxt, prv)


def _retention_kernel(q_ref, k_ref, v_ref, g_ref, cos_ref, sin_ref, mask_ref, dq_ref,
                      dk_ref, gb_ref, o_ref, st_ref):
    i = pl.program_id(0)

    @pl.when(i == 0)
    def _():
        st_ref[...] = jnp.zeros_like(st_ref)

    cos = cos_ref[...]
    sin = sin_ref[...]
    hs = range(A_HEADS)

    def rot(ref, h):
        x = ref[:, h * A_DK:(h + 1) * A_DK].astype(F32)
        return x * cos + _swap_pairs(x) * sin

    q = [rot(q_ref, h) for h in hs]
    k = [rot(k_ref, h) * (A_DK ** -0.5) for h in hs]
    v = [v_ref[:, h * A_DV:(h + 1) * A_DV].astype(BF16) for h in hs]
    st = [st_ref[h] for h in hs]
    sc = [_dot_nt(q[h].astype(BF16), k[h].astype(BF16)) * mask_ref[h] for h in hs]
    oi = [_dot((q[h] * dq_ref[h]).astype(BF16), st[h].astype(BF16)) for h in hs]
    o = [_dot(sc[h].astype(BF16), v[h]) + oi[h] for h in hs]
    kv = [_dot_tn((k[h] * dk_ref[h]).astype(BF16), v[h]) for h in hs]
    for h in hs:
        st_ref[h] = gb_ref[h] * st[h] + kv[h]
        on = o[h] * lax.rsqrt(jnp.mean(o[h] * o[h], axis=-1, keepdims=True) + A_NORM_EPS)
        cols = slice(h * A_DV, (h + 1) * A_DV)
        o_ref[:, cols] = (on * _silu(g_ref[:, cols].astype(F32))).astype(o_ref.dtype)


def _retention_tables():
    h = np.arange(A_HEADS, dtype=np.float64)
    log_g = np.log(1.0 - 2.0 ** (-5.0 - h))
    pos = np.arange(A_TB)
    d = pos[:, None] - pos[None, :]
    cq = pos[:, None] // CHUNK
    ck = pos[None, :] // CHUNK
    same = cq == ck
    past = ck < cq
    expo = np.where(same, np.abs(d), np.where(past, d, 0)).astype(np.float64)
    mask = np.exp(log_g[:, None, None] * expo[None]) * (same | past)[None]
    dq = np.exp(log_g[:, None] * (pos + 1.0)[None, :])
    dk = np.exp(log_g[:, None] * (A_TB - 1.0 - pos)[None, :])
    gb = np.exp(log_g * A_TB)
    ones = np.ones((1, 1, LANES))
    return (jnp.asarray(mask, F32), jnp.asarray(dq[:, :, None] * ones, F32),
            jnp.asarray(dk[:, :, None] * ones, F32),
            jnp.asarray(gb[:, None, None] * np.ones((1, 1, A_DV)), F32))


def _rotary_tables(s):
    inv = 1.0 / (ROPE_BASE ** (jnp.arange(0, A_DK, 2, dtype=F32) / A_DK))
    ang = jnp.arange(s).astype(F32)[:, None] * inv[None, :]
    cos = jnp.repeat(jnp.cos(ang), 2, axis=-1)
    sin = jnp.repeat(jnp.sin(ang), 2, axis=-1)
    sign = jnp.asarray(np.tile(np.array([-1.0, 1.0]), A_DK // 2), F32)
    return cos, sin * sign[None, :]


def retention(ha, rot_tbl, ret_tbl, out_dtype):
    s = ha.shape[0]
    cos, sin = rot_tbl
    mask, dq, dk, gb = ret_tbl
    whole = lambda shape: pl.BlockSpec(shape, lambda i: (0,) * len(shape))
    return pl.pallas_call(
        _retention_kernel,
        grid=(s // A_TB,),
        in_specs=[pl.BlockSpec((A_TB, A_QK), lambda i: (i, 0)),
                  pl.BlockSpec((A_TB, A_QK), lambda i: (i, 1)),
                  pl.BlockSpec((A_TB, A_W), lambda i: (i, 2 * A_QK // A_W)),
                  pl.BlockSpec((A_TB, A_W), lambda i: (i, 2 * A_QK // A_W + 1)),
                  pl.BlockSpec((A_TB, A_DK), lambda i: (i, 0)),
                  pl.BlockSpec((A_TB, A_DK), lambda i: (i, 0)),
                  whole((A_HEADS, A_TB, A_TB)), whole((A_HEADS, A_TB, LANES)),
                  whole((A_HEADS, A_TB, LANES)), whole((A_HEADS, 1, A_DV))],
        out_specs=pl.BlockSpec((A_TB, A_W), lambda i: (i, 0)),
        out_shape=jax.ShapeDtypeStruct((s, A_W), out_dtype),
        scratch_shapes=[pltpu.VMEM((A_HEADS, A_DK, A_DV), F32)],
        compiler_params=_params(("arbitrary",)),
        name="retention",
    )(ha, ha, ha, ha, cos, sin, mask, dq, dk, gb)


B_TP = 512
B_TC = 512
B_NCB = B_TC // CHUNK
B_NPAIR = B_W // LANES


def _seg_ones():
    r = lax.broadcasted_iota(jnp.int32, (LANES, LANES), 0) // B_HEAD
    c = lax.broadcasted_iota(jnp.int32, (LANES, LANES), 1) // B_HEAD
    return (r == c).astype(BF16)


def _seg_sum(x):
    ones = _seg_ones()
    hi = x.astype(BF16)
    lo = (x - hi.astype(F32)).astype(BF16)
    parts = [_dot(hi[:, j:j + LANES], ones) + _dot(lo[:, j:j + LANES], ones)
             for j in range(0, x.shape[1], LANES)]
    return jnp.concatenate(parts, axis=1)


def _rwkv_prep_kernel(hb_ref, prev_ref, mu_ref, w0_ref, wup_ref, a0_ref, aup_ref, kk_ref,
                      ka_ref, rk_ref, rt_ref, kq_ref, bt_ref, kt_ref, v_ref, bte_ref,
                      kte_ref, bonus_ref, pl_ref):
    i = pl.program_id(0)
    hb = hb_ref[...]
    rows = lax.broadcasted_iota(jnp.int32, hb.shape, 0)
    prev_row = jnp.where(i == 0, 0.0, prev_ref[7:8, :])
    shifted = jnp.where(rows == 0, prev_row, pltpu.roll(hb, 1, 0))
    xs = hb + (shifted - hb) * mu_ref[...]
    r = xs[:, 0:B_W]
    k = xs[:, B_W:2 * B_W]
    v = xs[:, 2 * B_W:3 * B_W]
    wd = xs[:, 3 * B_W:3 * B_W + B_RANK]
    ad = xs[:, 3 * B_W + B_RANK:3 * B_W + 2 * B_RANK]
    z = w0_ref[...] + _mm(jnp.tanh(wd), wup_ref[...])
    lw = -math.exp(-0.5) * jax.nn.sigmoid(z)
    a = jax.nn.sigmoid(a0_ref[...] + _mm(ad, aup_ref[...]))
    kk = k * kk_ref[...]
    nrm = jnp.maximum(jnp.sqrt(_seg_sum(kk * kk)), 1e-12)
    kk = kk / nrm
    kmod = k * (1.0 + (a - 1.0) * ka_ref[...])
    bonus_ref[...] = _seg_sum(r * kmod * rk_ref[...]) * v
    tr = lax.broadcasted_iota(jnp.int32, (B_TP, B_TP), 0)
    tc = lax.broadcasted_iota(jnp.int32, (B_TP, B_TP), 1)
    tri = ((tr // CHUNK == tc // CHUNK) & (tr >= tc)).astype(BF16)
    hi = lw.astype(BF16)
    rest = lw - hi.astype(F32)
    mid = rest.astype(BF16)
    lo = (rest - mid.astype(F32)).astype(BF16)
    cum = _dot(tri, hi) + _dot(tri, mid) + _dot(tri, lo)
    nc = B_TP // CHUNK
    ends = [cum[(c + 1) * CHUNK - 1:(c + 1) * CHUNK, :] for c in range(nc)]
    tot = jnp.concatenate([jnp.broadcast_to(e, (CHUNK, B_W)) for e in ends], axis=0)
    p_inv = jnp.exp(-cum)
    p_end = jnp.exp(tot - cum)
    rt_ref[...] = (r * jnp.exp(cum)).astype(BF16)
    kq_ref[...] = (kk * jnp.exp(cum - lw)).astype(BF16)
    bt_ref[...] = (kk * a * p_inv).astype(BF16)
    kt_ref[...] = (kmod * p_inv).astype(BF16)
    v_ref[...] = v.astype(BF16)
    bte_ref[...] = (kk * a * p_end).astype(BF16)
    kte_ref[...] = (kmod * p_end).astype(BF16)
    pl_ref[...] = jnp.exp(jnp.concatenate(ends, axis=0))


def rwkv_prep(hb, mu, w0, w_up, a0, a_up, k_k, k_a, r_k):
    s = hb.shape[0]
    nb = s // B_TP
    vec = lambda n: pl.BlockSpec((1, n), lambda i: (0, 0))
    full = lambda a, b: pl.BlockSpec((a, b), lambda i: (0, 0))
    big = pl.BlockSpec((B_TP, B_W), lambda i: (i, 0))
    outs = ([jax.ShapeDtypeStruct((s, B_W), BF16)] * 7
            + [jax.ShapeDtypeStruct((s, B_W), F32),
               jax.ShapeDtypeStruct((s // CHUNK, B_W), F32)])
    return pl.pallas_call(
        _rwkv_prep_kernel,
        grid=(nb,),
        in_specs=[pl.BlockSpec((B_TP, B_SHIFT_W), lambda i: (i, 0)),
                  pl.BlockSpec((8, B_SHIFT_W),
                               lambda i: (jnp.maximum(i * (B_TP // 8) - 1, 0), 0)),
                  vec(B_SHIFT_W), vec(B_W), full(B_RANK, B_W), vec(B_W), full(B_RANK, B_W),
                  vec(B_W), vec(B_W), vec(B_W)],
        out_specs=[big] * 8 + [pl.BlockSpec((B_TP // CHUNK, B_W), lambda i: (i, 0))],
        out_shape=outs,
        compiler_params=_params(("parallel",)),
        name="rwkv_prep",
    )(hb, hb, mu.reshape(1, -1), w0.reshape(1, -1), w_up, a0.reshape(1, -1), a_up,
      k_k.reshape(1, -1), k_a.reshape(1, -1), r_k.reshape(1, -1))


def _stack_heads(x):
    lane = lax.broadcasted_iota(jnp.int32, x.shape, 1)
    zero = jnp.zeros_like(x)
    return jnp.concatenate([jnp.where(lane < B_HEAD, x, zero),
                            jnp.where(lane >= B_HEAD, x, zero)], axis=0)


def _mm(a, b):
    return _dot(a.astype(BF16), b.astype(BF16))


def _rwkv_intra_kernel(rt_ref, kq_ref, bt_ref, kt_ref, v_ref, bte_ref, kte_ref, pl_ref,
                       mz_ref, gz_ref, rq_ref, yc_ref):
    n2 = 2 * CHUNK
    r_i = lax.broadcasted_iota(jnp.int32, (n2, n2), 0)
    c_i = lax.broadcasted_iota(jnp.int32, (n2, n2), 1)
    same_head = (r_i // CHUNK) == (c_i // CHUNK)
    strict = (same_head & (r_i > c_i)).astype(F32)
    incl = (same_head & (r_i >= c_i)).astype(F32)
    eye = (r_i == c_i).astype(F32)
    bd16 = ((r_i // 16) == (c_i // 16)).astype(F32)
    bd32 = ((r_i // 32) == (c_i // 32)).astype(F32)

    cs = range(B_NCB)

    def load(ref):
        return [_stack_heads(ref[c * CHUNK:(c + 1) * CHUNK, :]) for c in cs]

    def each(fn, *lists):
        return [fn(*args) for args in zip(*lists)]

    rt, kq, bt, kt, vv, bte, kte = (load(r) for r in (rt_ref, kq_ref, bt_ref, kt_ref, v_ref,
                                                      bte_ref, kte_ref))
    ab = each(lambda kq_, rt_, bt_, kt_: _dot_nt(jnp.concatenate([kq_, rt_], axis=0),
                                                 jnp.concatenate([bt_, kt_], axis=0)),
              kq, rt, bt, kt)
    a_m = [x[:n2, :n2] * strict for x in ab]
    b_m = [x[:n2, n2:] * strict for x in ab]
    aq = [x[n2:, :n2] * incl for x in ab]
    bq = [x[n2:, n2:] * incl for x in ab]
    nn = [-(x * bd16) for x in a_m]
    t = [eye + x for x in nn]
    sq = each(_mm, nn, nn)
    for step in range(3):
        t = each(lambda t_, s_: t_ + _mm(t_, s_), t, sq)
        if step < 2:
            sq = each(_mm, sq, sq)
    for blk in (bd32 - bd16, 1.0 - bd32):
        e = each(lambda a_, t_: _mm(a_ * blk, t_), a_m, t)
        t = each(lambda t_, e_: t_ - _mm(t_, e_), t, e)
    bv = each(_mm, b_m, vv)
    tw = each(lambda t_, kq_, bv_: _mm(t_, jnp.concatenate([kq_.astype(F32), bv_], axis=1)),
              t, kq, bv)
    w = [x[:, :n2].astype(BF16) for x in tw]
    u0 = [(-x[:, n2:]).astype(BF16) for x in tw]
    aw = each(lambda aq_, w_, u_: _mm(aq_, jnp.concatenate([w_, u_], axis=1)), aq, w, u0)
    bqv = each(_mm, bq, vv)
    wb = each(_dot_tn, w, bte)
    gz = each(lambda u_, v_, b_, k_: _dot_tn(jnp.concatenate([u_, v_], axis=0),
                                             jnp.concatenate([b_, k_], axis=0)),
              u0, vv, bte, kte)
    for c in cs:
        rq_ref[0, c] = (rt[c].astype(F32) - aw[c][:, :n2]).astype(BF16)
        yc2 = bqv[c] + aw[c][:, n2:]
        yc_ref[c * CHUNK:(c + 1) * CHUNK, :] = yc2[:CHUNK] + yc2[CHUNK:]
        mz_ref[0, c] = (eye * pl_ref[c:c + 1, :] - wb[c]).astype(BF16)
        gz_ref[0, c] = gz[c]


def rwkv_intra(prep):
    rt, kq, bt, kt, v, bte, kte, _, p_end = prep
    s = rt.shape[0]
    nchunk = s // CHUNK
    blk = pl.BlockSpec((B_TC, LANES), lambda p, i: (i, p))
    mat = pl.BlockSpec((1, B_NCB, LANES, LANES), lambda p, i: (p, i, 0, 0))
    mshape = (B_NPAIR, nchunk, LANES, LANES)
    return pl.pallas_call(
        _rwkv_intra_kernel,
        grid=(B_NPAIR, s // B_TC),
        in_specs=[blk] * 7 + [pl.BlockSpec((B_NCB, LANES), lambda p, i: (i, p))],
        out_specs=[mat, mat, mat, blk],
        out_shape=[jax.ShapeDtypeStruct(mshape, BF16), jax.ShapeDtypeStruct(mshape, F32),
                   jax.ShapeDtypeStruct(mshape, BF16), jax.ShapeDtypeStruct((s, B_W), F32)],
        compiler_params=_params(("parallel", "parallel")),
        name="rwkv_intra",
    )(rt, kq, bt, kt, v, bte, kte, p_end)


def _rwkv_serial_kernel(mz_ref, gz_ref, rq_ref, yc_ref, bonus_ref, g_ref, lng_ref, lnb_ref,
                        o_ref, zt_ref, y_ref):
    i = pl.program_id(0)

    @pl.when(i == 0)
    def _():
        zt_ref[...] = jnp.zeros_like(zt_ref)

    def chunk(c, carry):
        rows = pl.ds(pl.multiple_of(c * CHUNK, CHUNK), CHUNK)
        for p in range(B_NPAIR):
            cols = slice(p * LANES, (p + 1) * LANES)
            zt = zt_ref[p].astype(BF16)
            y2 = _dot_nt(rq_ref[p, c], zt)
            zt_ref[p] = _dot(zt, mz_ref[p, c]) + gz_ref[p, c]
            y_ref[rows, cols] = y2[:CHUNK] + y2[CHUNK:] + yc_ref[rows, cols]
        return carry

    lax.fori_loop(0, B_NCB, chunk, 0)
    y = y_ref[...]
    mean = _seg_sum(y) * (1.0 / B_HEAD)
    yc = y - mean
    var = _seg_sum(yc * yc) * (1.0 / B_HEAD)
    yn = yc * lax.rsqrt(var + B_LN_EPS) * lng_ref[...] + lnb_ref[...]
    o_ref[...] = ((yn + bonus_ref[...]) * _silu(g_ref[...].astype(F32))).astype(o_ref.dtype)


def rwkv_serial(intra, bonus, gate, ln_g, ln_b, out_dtype):
    mz, gz, rq, yc = intra
    s = yc.shape[0]
    mat = pl.BlockSpec((B_NPAIR, B_NCB, LANES, LANES), lambda i: (0, i, 0, 0))
    blk = pl.BlockSpec((B_TC, B_W), lambda i: (i, 0))
    vec = pl.BlockSpec((1, B_W), lambda i: (0, 0))
    return pl.pallas_call(
        _rwkv_serial_kernel,
        grid=(s // B_TC,),
        in_specs=[mat, mat, mat, blk, blk, blk, vec, vec],
        out_specs=blk,
        out_shape=jax.ShapeDtypeStruct((s, B_W), out_dtype),
        scratch_shapes=[pltpu.VMEM((B_NPAIR, LANES, LANES), F32),
                        pltpu.VMEM((B_TC, B_W), F32)],
        compiler_params=_params(("arbitrary",)),
        name="rwkv_serial",
    )(mz, gz, rq, yc, bonus, gate, ln_g.reshape(1, -1), ln_b.reshape(1, -1))


def rwkv7(hb, gate, mu, w0, w_up, a0, a_up, k_k, k_a, r_k, ln_g, ln_b, out_dtype):
    prep = rwkv_prep(hb, mu, w0, w_up, a0, a_up, k_k, k_a, r_k)
    intra = rwkv_intra(prep)
    return rwkv_serial(intra, prep[7], gate, ln_g, ln_b, out_dtype)


def kernel(x, norm_g, w_in, w_out, b_mu, b_w0, b_w_up, b_a0, b_a_up, b_k_k, b_k_a, b_r_k,
           b_ln_g, b_ln_b, c_rel_bias, final_g):
    bsz, s, d = x.shape
    assert bsz == 1 and d == D_MODEL
    xs = x.reshape(s, d)
    rot_tbl = _rotary_tables(s)
    ret_tbl = _retention_tables()
    bias_tbl = _attn_bias_tables(c_rel_bias)
    b0 = A_IN
    g0 = A_IN + B_SHIFT_W
    c0 = A_IN + B_IN
    cg = c0 + 3 * C_W
    qscale = jnp.asarray(np.concatenate([np.full(C_W, C_QSCALE), np.ones(2 * C_W)])[None, :],
                         F32)
    for l in range(DEPTH):
        xn = rmsnorm(xs, norm_g[l], BF16)
        ha = proj(xn, w_in, l, 0, b0, F32, "proj_a")
        hb = proj(xn, w_in, l, b0, g0 - b0, F32, "proj_b")
        gb = proj(xn, w_in, l, g0, c0 - g0, F32, "proj_gb")
        qkv = proj(xn, w_in, l, c0, cg - c0, BF16, "proj_c", col_scale=qscale)
        gc = proj(xn, w_in, l, cg, C_W, F32, "proj_gc")
        ya = retention(ha, rot_tbl, ret_tbl, BF16)
        yb = rwkv7(hb, gb, b_mu[l], b_w0[l], b_w_up[l], b_a0[l], b_a_up[l], b_k_k[l],
                   b_k_a[l], b_r_k[l].reshape(-1), b_ln_g[l], b_ln_b[l], BF16)
        yc = chunk_attention(qkv, gc, bias_tbl, l, BF16)
        xs = outproj(ya, yb, yc, w_out, l, xs)
    return rmsnorm(xs, final_g, F32).reshape(bsz, s, d)
```

```python
import math

import numpy as np
import jax
import jax.numpy as jnp
from jax import lax
from jax.experimental import pallas as pl
from jax.experimental.pallas import tpu as pltpu

F32 = jnp.float32
BF16 = jnp.bfloat16

D_MODEL = 4096
DEPTH = 4
CHUNK = 64
A_W = 1024
A_HEADS = 4
A_DV = 256
A_DK = 128
A_QK = 512
A_IN = 2 * A_QK + 2 * A_W
ROPE_BASE = 10000.0
A_NORM_EPS = 1e-6
B_W = 1024
B_HEAD = 64
B_HEADS = 16
B_RANK = 128
B_SHIFT_W = 3 * B_W + 2 * B_RANK
B_LN_EPS = 64e-5
B_IN = B_SHIFT_W + B_W
C_W = 2048
C_HEADS = 16
C_DH = 128
C_BACK = 8
C_REL_CLIP = 128
RMS_EPS = 1e-6

LANES = 128
VMEM_LIMIT = 56 * 1024 * 1024

NEG = -1e30


def _params(sem):
    return pltpu.CompilerParams(dimension_semantics=sem, vmem_limit_bytes=VMEM_LIMIT)


def _dot(a, b, prec=None):
    return jnp.dot(a, b, preferred_element_type=F32, precision=prec)


def _dot_nt(a, b, prec=None):
    return lax.dot_general(a, b, (((1,), (1,)), ((), ())), preferred_element_type=F32,
                           precision=prec)


def _dot_tn(a, b, prec=None):
    return lax.dot_general(a, b, (((0,), (0,)), ((), ())), preferred_element_type=F32,
                           precision=prec)


def _silu(g):
    return g * jax.nn.sigmoid(g)


def _rmsnorm_kernel(x_ref, g_ref, o_ref):
    x = x_ref[...]
    ms = jnp.mean(x * x, axis=-1, keepdims=True)
    o_ref[...] = (x * lax.rsqrt(ms + RMS_EPS) * g_ref[...]).astype(o_ref.dtype)


def rmsnorm(x, g, out_dtype, tm=256):
    s, d = x.shape
    return pl.pallas_call(
        _rmsnorm_kernel,
        grid=(s // tm,),
        in_specs=[pl.BlockSpec((tm, d), lambda i: (i, 0)),
                  pl.BlockSpec((1, d), lambda i: (0, 0))],
        out_specs=pl.BlockSpec((tm, d), lambda i: (i, 0)),
        out_shape=jax.ShapeDtypeStruct((s, d), out_dtype),
        compiler_params=_params(("parallel",)),
        name="rmsnorm",
    )(x, g.reshape(1, d))


P_TM = 2048
P_TN = 256


def _resident_rows(shape):
    return pl.BlockSpec(shape, lambda i, j: (i, 0), pipeline_mode=pl.Buffered(1))


def _proj_segments():
    b0 = A_IN
    g0 = b0 + B_SHIFT_W
    c0 = g0 + B_W
    return ((0, A_IN, 0, None), (b0, B_SHIFT_W, 1, None), (g0, B_W, 2, None),
            (c0, C_W, 3, C_QSCALE), (c0 + C_W, 2 * C_W, 3, None), (c0 + 3 * C_W, C_W, 4, None))


def _inproj_kernel(a_ref, w_ref, *o_refs):
    j = pl.program_id(1)
    for col0, width, slot, scale in _proj_segments():
        lo, hi = col0 // P_TN, (col0 + width) // P_TN

        @pl.when((j >= lo) & (j < hi))
        def _(slot=slot, scale=scale):
            acc = _dot(a_ref[...], w_ref[0].astype(BF16))
            if scale is not None:
                acc = acc * scale
            o_refs[slot][...] = acc.astype(o_refs[slot].dtype)


def inproj(a, w_all, layer):
    m, k = a.shape
    segs = _proj_segments()
    assert m % P_TM == 0 and all(c % P_TN == 0 and w % P_TN == 0 for c, w, _, _ in segs)
    dtypes = (F32, F32, F32, BF16, F32)
    out_specs, out_shape = [], []
    for slot, dt in enumerate(dtypes):
        lo = min(c for c, _, s, _ in segs if s == slot) // P_TN
        n = sum(w for _, w, s, _ in segs if s == slot) // P_TN
        out_specs.append(pl.BlockSpec(
            (P_TM, P_TN), lambda i, j, lo=lo, n=n: (i, jnp.clip(j - lo, 0, n - 1))))
        out_shape.append(jax.ShapeDtypeStruct((m, n * P_TN), dt))
    return pl.pallas_call(
        _inproj_kernel,
        grid=(m // P_TM, w_all.shape[2] // P_TN),
        in_specs=[_resident_rows((P_TM, k)),
                  pl.BlockSpec((1, k, P_TN), lambda i, j: (layer, 0, j))],
        out_specs=out_specs,
        out_shape=out_shape,
        compiler_params=_params(("parallel", "arbitrary")),
        name="inproj",
    )(a, w_all)


def _outproj_kernel(ya_ref, yb_ref, yc_ref, wa_ref, wb_ref, wc_ref, x_ref, o_ref):
    acc = _dot(ya_ref[...], wa_ref[0].astype(BF16))
    acc += _dot(yb_ref[...], wb_ref[0].astype(BF16))
    acc += _dot(yc_ref[...], wc_ref[0].astype(BF16))
    o_ref[...] = x_ref[...] + acc


def outproj(ya, yb, yc, w_all, layer, x):
    s = x.shape[0]
    n = w_all.shape[2]
    tm, tn = P_TM, P_TN
    return pl.pallas_call(
        _outproj_kernel,
        grid=(s // tm, n // tn),
        in_specs=[_resident_rows((tm, A_W)), _resident_rows((tm, B_W)),
                  _resident_rows((tm, C_W)),
                  pl.BlockSpec((1, A_W, tn), lambda i, j: (layer, 0, j)),
                  pl.BlockSpec((1, B_W, tn), lambda i, j: (layer, A_W // B_W, j)),
                  pl.BlockSpec((1, C_W, tn), lambda i, j: (layer, (A_W + B_W) // C_W, j)),
                  pl.BlockSpec((tm, tn), lambda i, j: (i, j))],
        out_specs=pl.BlockSpec((tm, tn), lambda i, j: (i, j)),
        out_shape=jax.ShapeDtypeStruct((s, n), F32),
        compiler_params=_params(("parallel", "parallel")),
        name="outproj",
    )(ya, yb, yc, w_all, w_all, w_all, x)


C_TQ = 256
C_NKB = C_BACK * CHUNK // C_TQ + 1
C_TK = C_NKB * C_TQ
C_TROW = C_TQ + C_TK


C_HPS = 4
LOG2E = math.log2(math.e)
C_QSCALE = C_DH ** -0.5 * LOG2E


def _attn_kernel(q_ref, k0_ref, k1_ref, k2_ref, v0_ref, v1_ref, v2_ref, g_ref, bias_ref,
                 o_ref):
    heads = [slice(h * C_DH, (h + 1) * C_DH) for h in range(C_HPS)]
    s = [_dot_nt(q_ref[:, c],
                 jnp.concatenate([k0_ref[:, c], k1_ref[:, c], k2_ref[:, c]], axis=0))
         + bias_ref[h] for h, c in enumerate(heads)]
    p = [jnp.exp2(x - jnp.max(x, axis=-1, keepdims=True)) for x in s]
    l = [jnp.sum(x, axis=-1, keepdims=True) for x in p]
    o = [_dot(x.astype(BF16),
              jnp.concatenate([v0_ref[:, c], v1_ref[:, c], v2_ref[:, c]], axis=0))
         for x, c in zip(p, heads)]
    for h, c in enumerate(heads):
        o_ref[:, c] = (o[h] / l[h] * _silu(g_ref[:, c].astype(F32))).astype(o_ref.dtype)


def _bias_kernel(r_ref, o_ref):
    x = jnp.broadcast_to(r_ref[0], (C_TQ, C_TROW))
    y = pltpu.roll(x, 0, 1, stride=1, stride_axis=0)[:, :C_TK]
    qc = lax.broadcasted_iota(jnp.int32, (C_TQ, C_TK), 0) // CHUNK
    kcol = lax.broadcasted_iota(jnp.int32, (C_TQ, C_TK), 1)
    dchunk = qc - (kcol // CHUNK - (C_TK - C_TQ) // CHUNK)
    band = (dchunk >= 0) & (dchunk <= C_BACK)
    for v in range(C_NKB):
        o_ref[v, 0] = jnp.where(band & (kcol >= (C_NKB - 1 - v) * C_TQ), y, NEG)


def _attn_bias_tables(rel_bias):
    j = np.arange(C_TROW)
    dist = (C_TK - j) % C_TROW - C_TQ
    idx = np.clip(dist, -(CHUNK - 1), C_REL_CLIP) + (CHUNK - 1)
    nl, nh = rel_bias.shape[0], rel_bias.shape[1]
    row = jnp.take(rel_bias, jnp.asarray(idx, jnp.int32), axis=-1) * LOG2E
    return pl.pallas_call(
        _bias_kernel,
        grid=(nl, nh),
        in_specs=[pl.BlockSpec((1, 1, C_TROW), lambda l, h: (l * nh + h, 0, 0))],
        out_specs=pl.BlockSpec((C_NKB, 1, C_TQ, C_TK), lambda l, h: (l, h, 0, 0)),
        out_shape=jax.ShapeDtypeStruct((nl * C_NKB, nh, C_TQ, C_TK), F32),
        compiler_params=_params(("parallel", "parallel")),
        name="attn_bias",
    )(row.reshape(nl * nh, 1, C_TROW))


def chunk_attention(qkv, gate, bias_tbl, layer, out_dtype):
    s = qkv.shape[0]
    nq = s // C_TQ
    wblk = C_HPS * C_DH
    ng = C_HEADS // C_HPS

    def kv_spec(base, back):
        return pl.BlockSpec((C_TQ, wblk),
                            lambda h, i: (jnp.maximum(i - back, 0), base + h))

    def bias_map(h, i):
        return (layer * C_NKB + jnp.minimum(i, C_NKB - 1), h, 0, 0)

    return pl.pallas_call(
        _attn_kernel,
        grid=(ng, nq),
        in_specs=[pl.BlockSpec((C_TQ, wblk), lambda h, i: (i, h)),
                  kv_spec(ng, 2), kv_spec(ng, 1), kv_spec(ng, 0),
                  kv_spec(2 * ng, 2), kv_spec(2 * ng, 1), kv_spec(2 * ng, 0),
                  pl.BlockSpec((C_TQ, wblk), lambda h, i: (i, h)),
                  pl.BlockSpec((None, C_HPS, C_TQ, C_TK), bias_map)],
        out_specs=pl.BlockSpec((C_TQ, wblk), lambda h, i: (i, h)),
        out_shape=jax.ShapeDtypeStruct((s, C_W), out_dtype),
        compiler_params=_params(("parallel", "parallel")),
        name="chunk_attn",
    )(qkv, qkv, qkv, qkv, qkv, qkv, qkv, gate, bias_tbl)


A_TB = 256


def _swap_pairs(x):
    lane = lax.broadcasted_iota(jnp.int32, x.shape, x.ndim - 1)
    nxt = pltpu.roll(x, LANES - 1, x.ndim - 1)
    prv = pltpu.roll(x, 1, x.ndim - 1)
    return jnp.where(lane % 2 == 0, nxt, prv)


def _retention_kernel(q_ref, k_ref, v_ref, g_ref, cos_ref, sin_ref, mask_ref, dq_ref,
                      dk_ref, gb_ref, o_ref, st_ref):
    i = pl.program_id(0)

    @pl.when(i == 0)
    def _():
        st_ref[...] = jnp.zeros_like(st_ref)

    cos = cos_ref[...]
    sin = sin_ref[...]
    hs = range(A_HEADS)

    def rot(ref, h):
        x = ref[:, h * A_DK:(h + 1) * A_DK].astype(F32)
        return x * cos + _swap_pairs(x) * sin

    q = [rot(q_ref, h) for h in hs]
    k = [rot(k_ref, h) * (A_DK ** -0.5) for h in hs]
    v = [v_ref[:, h * A_DV:(h + 1) * A_DV].astype(BF16) for h in hs]
    st = [st_ref[h] for h in hs]
    sc = [_dot_nt(q[h].astype(BF16), k[h].astype(BF16)) * mask_ref[h] for h in hs]
    oi = [_dot((q[h] * dq_ref[h]).astype(BF16), st[h].astype(BF16)) for h in hs]
    o = [_dot(sc[h].astype(BF16), v[h]) + oi[h] for h in hs]
    kv = [_dot_tn((k[h] * dk_ref[h]).astype(BF16), v[h]) for h in hs]
    for h in hs:
        st_ref[h] = gb_ref[h] * st[h] + kv[h]
        on = o[h] * lax.rsqrt(jnp.mean(o[h] * o[h], axis=-1, keepdims=True) + A_NORM_EPS)
        cols = slice(h * A_DV, (h + 1) * A_DV)
        o_ref[:, cols] = (on * _silu(g_ref[:, cols].astype(F32))).astype(o_ref.dtype)


def _retention_tables():
    h = np.arange(A_HEADS, dtype=np.float64)
    log_g = np.log(1.0 - 2.0 ** (-5.0 - h))
    pos = np.arange(A_TB)
    d = pos[:, None] - pos[None, :]
    cq = pos[:, None] // CHUNK
    ck = pos[None, :] // CHUNK
    same = cq == ck
    past = ck < cq
    expo = np.where(same, np.abs(d), np.where(past, d, 0)).astype(np.float64)
    mask = np.exp(log_g[:, None, None] * expo[None]) * (same | past)[None]
    dq = np.exp(log_g[:, None] * (pos + 1.0)[None, :])
    dk = np.exp(log_g[:, None] * (A_TB - 1.0 - pos)[None, :])
    gb = np.exp(log_g * A_TB)
    ones = np.ones((1, 1, LANES))
    return (jnp.asarray(mask, F32), jnp.asarray(dq[:, :, None] * ones, F32),
            jnp.asarray(dk[:, :, None] * ones, F32),
            jnp.asarray(gb[:, None, None] * np.ones((1, 1, A_DV)), F32))


def _rotary_tables(s):
    inv = 1.0 / (ROPE_BASE ** (jnp.arange(0, A_DK, 2, dtype=F32) / A_DK))
    ang = jnp.arange(s).astype(F32)[:, None] * inv[None, :]
    cos = jnp.repeat(jnp.cos(ang), 2, axis=-1)
    sin = jnp.repeat(jnp.sin(ang), 2, axis=-1)
    sign = jnp.asarray(np.tile(np.array([-1.0, 1.0]), A_DK // 2), F32)
    return cos, sin * sign[None, :]


def retention(ha, rot_tbl, ret_tbl, out_dtype):
    s = ha.shape[0]
    cos, sin = rot_tbl
    mask, dq, dk, gb = ret_tbl
    whole = lambda shape: pl.BlockSpec(shape, lambda i: (0,) * len(shape))
    return pl.pallas_call(
        _retention_kernel,
        grid=(s // A_TB,),
        in_specs=[pl.BlockSpec((A_TB, A_QK), lambda i: (i, 0)),
                  pl.BlockSpec((A_TB, A_QK), lambda i: (i, 1)),
                  pl.BlockSpec((A_TB, A_W), lambda i: (i, 2 * A_QK // A_W)),
                  pl.BlockSpec((A_TB, A_W), lambda i: (i, 2 * A_QK // A_W + 1)),
                  pl.BlockSpec((A_TB, A_DK), lambda i: (i, 0)),
                  pl.BlockSpec((A_TB, A_DK), lambda i: (i, 0)),
                  whole((A_HEADS, A_TB, A_TB)), whole((A_HEADS, A_TB, LANES)),
                  whole((A_HEADS, A_TB, LANES)), whole((A_HEADS, 1, A_DV))],
        out_specs=pl.BlockSpec((A_TB, A_W), lambda i: (i, 0)),
        out_shape=jax.ShapeDtypeStruct((s, A_W), out_dtype),
        scratch_shapes=[pltpu.VMEM((A_HEADS, A_DK, A_DV), F32)],
        compiler_params=_params(("arbitrary",)),
        name="retention",
    )(ha, ha, ha, ha, cos, sin, mask, dq, dk, gb)


B_TP = 512
B_TC = 512
B_NCB = B_TC // CHUNK
B_NPAIR = B_W // LANES


def _seg_ones():
    r = lax.broadcasted_iota(jnp.int32, (LANES, LANES), 0) // B_HEAD
    c = lax.broadcasted_iota(jnp.int32, (LANES, LANES), 1) // B_HEAD
    return (r == c).astype(BF16)


def _seg_sum(x):
    ones = _seg_ones()
    hi = x.astype(BF16)
    lo = (x - hi.astype(F32)).astype(BF16)
    parts = [_dot(hi[:, j:j + LANES], ones) + _dot(lo[:, j:j + LANES], ones)
             for j in range(0, x.shape[1], LANES)]
    return jnp.concatenate(parts, axis=1)


def _rwkv_prep_kernel(hb_ref, prev_ref, mu_ref, w0_ref, wup_ref, a0_ref, aup_ref, kk_ref,
                      ka_ref, rk_ref, rt_ref, kq_ref, bt_ref, kt_ref, v_ref, bte_ref,
                      kte_ref, bonus_ref, pl_ref):
    i = pl.program_id(0)
    hb = hb_ref[...]
    rows = lax.broadcasted_iota(jnp.int32, hb.shape, 0)
    prev_row = jnp.where(i == 0, 0.0, prev_ref[7:8, :])
    shifted = jnp.where(rows == 0, prev_row, pltpu.roll(hb, 1, 0))
    xs = hb + (shifted - hb) * mu_ref[...]
    r = xs[:, 0:B_W]
    k = xs[:, B_W:2 * B_W]
    v = xs[:, 2 * B_W:3 * B_W]
    wd = xs[:, 3 * B_W:3 * B_W + B_RANK]
    ad = xs[:, 3 * B_W + B_RANK:3 * B_W + 2 * B_RANK]
    z = w0_ref[...] + _mm(jnp.tanh(wd), wup_ref[...])
    lw = -math.exp(-0.5) * jax.nn.sigmoid(z)
    a = jax.nn.sigmoid(a0_ref[...] + _mm(ad, aup_ref[...]))
    kk = k * kk_ref[...]
    nrm = jnp.maximum(jnp.sqrt(_seg_sum(kk * kk)), 1e-12)
    kk = kk / nrm
    kmod = k * (1.0 + (a - 1.0) * ka_ref[...])
    bonus_ref[...] = _seg_sum(r * kmod * rk_ref[...]) * v
    tr = lax.broadcasted_iota(jnp.int32, (B_TP, B_TP), 0)
    tc = lax.broadcasted_iota(jnp.int32, (B_TP, B_TP), 1)
    tri = ((tr // CHUNK == tc // CHUNK) & (tr >= tc)).astype(BF16)
    hi = lw.astype(BF16)
    rest = lw - hi.astype(F32)
    mid = rest.astype(BF16)
    lo = (rest - mid.astype(F32)).astype(BF16)
    cum = _dot(tri, hi) + _dot(tri, mid) + _dot(tri, lo)
    nc = B_TP // CHUNK
    ends = [cum[(c + 1) * CHUNK - 1:(c + 1) * CHUNK, :] for c in range(nc)]
    tot = jnp.concatenate([jnp.broadcast_to(e, (CHUNK, B_W)) for e in ends], axis=0)
    p_inv = jnp.exp(-cum)
    p_end = jnp.exp(tot - cum)
    rt_ref[...] = (r * jnp.exp(cum)).astype(BF16)
    kq_ref[...] = (kk * jnp.exp(cum - lw)).astype(BF16)
    bt_ref[...] = (kk * a * p_inv).astype(BF16)
    kt_ref[...] = (kmod * p_inv).astype(BF16)
    v_ref[...] = v.astype(BF16)
    bte_ref[...] = (kk * a * p_end).astype(BF16)
    kte_ref[...] = (kmod * p_end).astype(BF16)
    pl_ref[...] = jnp.exp(jnp.concatenate(ends, axis=0))


def rwkv_prep(hb, mu, w0, w_up, a0, a_up, k_k, k_a, r_k):
    s = hb.shape[0]
    nb = s // B_TP
    vec = lambda n: pl.BlockSpec((1, n), lambda i: (0, 0))
    full = lambda a, b: pl.BlockSpec((a, b), lambda i: (0, 0))
    big = pl.BlockSpec((B_TP, B_W), lambda i: (i, 0))
    outs = ([jax.ShapeDtypeStruct((s, B_W), BF16)] * 7
            + [jax.ShapeDtypeStruct((s, B_W), F32),
               jax.ShapeDtypeStruct((s // CHUNK, B_W), F32)])
    return pl.pallas_call(
        _rwkv_prep_kernel,
        grid=(nb,),
        in_specs=[pl.BlockSpec((B_TP, B_SHIFT_W), lambda i: (i, 0)),
                  pl.BlockSpec((8, B_SHIFT_W),
                               lambda i: (jnp.maximum(i * (B_TP // 8) - 1, 0), 0)),
                  vec(B_SHIFT_W), vec(B_W), full(B_RANK, B_W), vec(B_W), full(B_RANK, B_W),
                  vec(B_W), vec(B_W), vec(B_W)],
        out_specs=[big] * 8 + [pl.BlockSpec((B_TP // CHUNK, B_W), lambda i: (i, 0))],
        out_shape=outs,
        compiler_params=_params(("parallel",)),
        name="rwkv_prep",
    )(hb, hb, mu.reshape(1, -1), w0.reshape(1, -1), w_up, a0.reshape(1, -1), a_up,
      k_k.reshape(1, -1), k_a.reshape(1, -1), r_k.reshape(1, -1))


def _stack_heads(x):
    lane = lax.broadcasted_iota(jnp.int32, x.shape, 1)
    zero = jnp.zeros_like(x)
    return jnp.concatenate([jnp.where(lane < B_HEAD, x, zero),
                            jnp.where(lane >= B_HEAD, x, zero)], axis=0)


def _mm(a, b):
    return _dot(a.astype(BF16), b.astype(BF16))


def _rwkv_intra_kernel(rt_ref, kq_ref, bt_ref, kt_ref, v_ref, bte_ref, kte_ref, pl_ref,
                       mz_ref, gz_ref, rq_ref, yc_ref):
    n2 = 2 * CHUNK
    r_i = lax.broadcasted_iota(jnp.int32, (n2, n2), 0)
    c_i = lax.broadcasted_iota(jnp.int32, (n2, n2), 1)
    same_head = (r_i // CHUNK) == (c_i // CHUNK)
    strict = (same_head & (r_i > c_i)).astype(F32)
    incl = (same_head & (r_i >= c_i)).astype(F32)
    eye = (r_i == c_i).astype(F32)
    bd16 = ((r_i // 16) == (c_i // 16)).astype(F32)
    bd32 = ((r_i // 32) == (c_i // 32)).astype(F32)

    cs = range(B_NCB)

    def load(ref):
        return [_stack_heads(ref[c * CHUNK:(c + 1) * CHUNK, :]) for c in cs]

    def each(fn, *lists):
        return [fn(*args) for args in zip(*lists)]

    rt, kq, bt, kt, vv, bte, kte = (load(r) for r in (rt_ref, kq_ref, bt_ref, kt_ref, v_ref,
                                                      bte_ref, kte_ref))
    ab = each(lambda kq_, rt_, bt_, kt_: _dot_nt(jnp.concatenate([kq_, rt_], axis=0),
                                                 jnp.concatenate([bt_, kt_], axis=0)),
              kq, rt, bt, kt)
    a_m = [x[:n2, :n2] * strict for x in ab]
    b_m = [x[:n2, n2:] * strict for x in ab]
    aq = [x[n2:, :n2] * incl for x in ab]
    bq = [x[n2:, n2:] * incl for x in ab]
    nn = [-(x * bd16) for x in a_m]
    t = [eye + x for x in nn]
    sq = each(_mm, nn, nn)
    for step in range(3):
        t = each(lambda t_, s_: t_ + _mm(t_, s_), t, sq)
        if step < 2:
            sq = each(_mm, sq, sq)
    for blk in (bd32 - bd16, 1.0 - bd32):
        e = each(lambda a_, t_: _mm(a_ * blk, t_), a_m, t)
        t = each(lambda t_, e_: t_ - _mm(t_, e_), t, e)
    bv = each(_mm, b_m, vv)
    tw = each(lambda t_, kq_, bv_: _mm(t_, jnp.concatenate([kq_.astype(F32), bv_], axis=1)),
              t, kq, bv)
    w = [x[:, :n2].astype(BF16) for x in tw]
    u0 = [(-x[:, n2:]).astype(BF16) for x in tw]
    aw = each(lambda aq_, w_, u_: _mm(aq_, jnp.concatenate([w_, u_], axis=1)), aq, w, u0)
    bqv = each(_mm, bq, vv)
    wb = each(_dot_tn, w, bte)
    gz = each(lambda u_, v_, b_, k_: _dot_tn(jnp.concatenate([u_, v_], axis=0),
                                             jnp.concatenate([b_, k_], axis=0)),
              u0, vv, bte, kte)
    for c in cs:
        rq_ref[0, c] = (rt[c].astype(F32) - aw[c][:, :n2]).astype(BF16)
        yc2 = bqv[c] + aw[c][:, n2:]
        yc_ref[c * CHUNK:(c + 1) * CHUNK, :] = yc2[:CHUNK] + yc2[CHUNK:]
        mz_ref[0, c] = (eye * pl_ref[c:c + 1, :] - wb[c]).astype(BF16)
        gz_ref[0, c] = gz[c]


def rwkv_intra(prep):
    rt, kq, bt, kt, v, bte, kte, _, p_end = prep
    s = rt.shape[0]
    nchunk = s // CHUNK
    blk = pl.BlockSpec((B_TC, LANES), lambda p, i: (i, p))
    mat = pl.BlockSpec((1, B_NCB, LANES, LANES), lambda p, i: (p, i, 0, 0))
    mshape = (B_NPAIR, nchunk, LANES, LANES)
    return pl.pallas_call(
        _rwkv_intra_kernel,
        grid=(B_NPAIR, s // B_TC),
        in_specs=[blk] * 7 + [pl.BlockSpec((B_NCB, LANES), lambda p, i: (i, p))],
        out_specs=[mat, mat, mat, blk],
        out_shape=[jax.ShapeDtypeStruct(mshape, BF16), jax.ShapeDtypeStruct(mshape, F32),
                   jax.ShapeDtypeStruct(mshape, BF16), jax.ShapeDtypeStruct((s, B_W), F32)],
        compiler_params=_params(("parallel", "parallel")),
        name="rwkv_intra",
    )(rt, kq, bt, kt, v, bte, kte, p_end)


def _rwkv_serial_kernel(mz_ref, gz_ref, rq_ref, yc_ref, bonus_ref, g_ref, lng_ref, lnb_ref,
                        o_ref, zt_ref, y_ref):
    i = pl.program_id(0)

    @pl.when(i == 0)
    def _():
        zt_ref[...] = jnp.zeros_like(zt_ref)

    def chunk(c, carry):
        rows = pl.ds(pl.multiple_of(c * CHUNK, CHUNK), CHUNK)
        for p in range(B_NPAIR):
            cols = slice(p * LANES, (p + 1) * LANES)
            zt = zt_ref[p].astype(BF16)
            y2 = _dot_nt(rq_ref[p, c], zt)
            zt_ref[p] = _dot(zt, mz_ref[p, c]) + gz_ref[p, c]
            y_ref[rows, cols] = y2[:CHUNK] + y2[CHUNK:] + yc_ref[rows, cols]
        return carry

    lax.fori_loop(0, B_NCB, chunk, 0)
    y = y_ref[...]
    mean = _seg_sum(y) * (1.0 / B_HEAD)
    yc = y - mean
    var = _seg_sum(yc * yc) * (1.0 / B_HEAD)
    yn = yc * lax.rsqrt(var + B_LN_EPS) * lng_ref[...] + lnb_ref[...]
    o_ref[...] = ((yn + bonus_ref[...]) * _silu(g_ref[...].astype(F32))).astype(o_ref.dtype)


def rwkv_serial(intra, bonus, gate, ln_g, ln_b, out_dtype):
    mz, gz, rq, yc = intra
    s = yc.shape[0]
    mat = pl.BlockSpec((B_NPAIR, B_NCB, LANES, LANES), lambda i: (0, i, 0, 0))
    blk = pl.BlockSpec((B_TC, B_W), lambda i: (i, 0))
    vec = pl.BlockSpec((1, B_W), lambda i: (0, 0))
    return pl.pallas_call(
        _rwkv_serial_kernel,
        grid=(s // B_TC,),
        in_specs=[mat, mat, mat, blk, blk, blk, vec, vec],
        out_specs=blk,
        out_shape=jax.ShapeDtypeStruct((s, B_W), out_dtype),
        scratch_shapes=[pltpu.VMEM((B_NPAIR, LANES, LANES), F32),
                        pltpu.VMEM((B_TC, B_W), F32)],
        compiler_params=_params(("arbitrary",)),
        name="rwkv_serial",
    )(mz, gz, rq, yc, bonus, gate, ln_g.reshape(1, -1), ln_b.reshape(1, -1))


def rwkv7(hb, gate, mu, w0, w_up, a0, a_up, k_k, k_a, r_k, ln_g, ln_b, out_dtype):
    prep = rwkv_prep(hb, mu, w0, w_up, a0, a_up, k_k, k_a, r_k)
    intra = rwkv_intra(prep)
    return rwkv_serial(intra, prep[7], gate, ln_g, ln_b, out_dtype)


def kernel(x, norm_g, w_in, w_out, b_mu, b_w0, b_w_up, b_a0, b_a_up, b_k_k, b_k_a, b_r_k,
           b_ln_g, b_ln_b, c_rel_bias, final_g):
    bsz, s, d = x.shape
    assert bsz == 1 and d == D_MODEL
    xs = x.reshape(s, d)
    rot_tbl = _rotary_tables(s)
    ret_tbl = _retention_tables()
    bias_tbl = _attn_bias_tables(c_rel_bias)
    for l in range(DEPTH):
        xn = rmsnorm(xs, norm_g[l], BF16)
        ha, hb, gb, qkv, gc = inproj(xn, w_in, l)
        ya = retention(ha, rot_tbl, ret_tbl, BF16)
        yb = rwkv7(hb, gb, b_mu[l], b_w0[l], b_w_up[l], b_a0[l], b_a_up[l], b_k_k[l],
                   b_k_a[l], b_r_k[l].reshape(-1), b_ln_g[l], b_ln_b[l], BF16)
        yc = chunk_attention(qkv, gc, bias_tbl, l, BF16)
        xs = outproj(ya, yb, yc, w_out, l, xs)
    return rmsnorm(xs, final_g, F32).reshape(bsz, s, d)
```

```python
import math

import numpy as np
import jax
import jax.numpy as jnp
from jax import lax
from jax.experimental import pallas as pl
from jax.experimental.pallas import tpu as pltpu

F32 = jnp.float32
BF16 = jnp.bfloat16

D_MODEL = 4096
DEPTH = 4
CHUNK = 64
A_W = 1024
A_HEADS = 4
A_DV = 256
A_DK = 128
A_QK = 512
A_IN = 2 * A_QK + 2 * A_W
ROPE_BASE = 10000.0
A_NORM_EPS = 1e-6
B_W = 1024
B_HEAD = 64
B_HEADS = 16
B_RANK = 128
B_SHIFT_W = 3 * B_W + 2 * B_RANK
B_LN_EPS = 64e-5
B_IN = B_SHIFT_W + B_W
C_W = 2048
C_HEADS = 16
C_DH = 128
C_BACK = 8
C_REL_CLIP = 128
RMS_EPS = 1e-6

LANES = 128
VMEM_LIMIT = 56 * 1024 * 1024

NEG = -1e30


def _params(sem):
    return pltpu.CompilerParams(dimension_semantics=sem, vmem_limit_bytes=VMEM_LIMIT)


def _dot(a, b, prec=None):
    return jnp.dot(a, b, preferred_element_type=F32, precision=prec)


def _dot_nt(a, b, prec=None):
    return lax.dot_general(a, b, (((1,), (1,)), ((), ())), preferred_element_type=F32,
                           precision=prec)


def _dot_tn(a, b, prec=None):
    return lax.dot_general(a, b, (((0,), (0,)), ((), ())), preferred_element_type=F32,
                           precision=prec)


def _silu(g):
    return g * jax.nn.sigmoid(g)


def _rmsnorm_kernel(x_ref, g_ref, o_ref):
    x = x_ref[...]
    ms = jnp.mean(x * x, axis=-1, keepdims=True)
    o_ref[...] = (x * lax.rsqrt(ms + RMS_EPS) * g_ref[...]).astype(o_ref.dtype)


def rmsnorm(x, g, out_dtype, tm=256):
    s, d = x.shape
    return pl.pallas_call(
        _rmsnorm_kernel,
        grid=(s // tm,),
        in_specs=[pl.BlockSpec((tm, d), lambda i: (i, 0)),
                  pl.BlockSpec((1, d), lambda i: (0, 0))],
        out_specs=pl.BlockSpec((tm, d), lambda i: (i, 0)),
        out_shape=jax.ShapeDtypeStruct((s, d), out_dtype),
        compiler_params=_params(("parallel",)),
        name="rmsnorm",
    )(x, g.reshape(1, d))


def _lane_partial_sq(x):
    sq = x * x
    acc = sq[:, 0:LANES]
    for j in range(LANES, x.shape[1], LANES):
        acc = acc + sq[:, j:j + LANES]
    return acc


def _prenorm_kernel(x_ref, g_ref, xg_ref, ssq_ref):
    x = x_ref[...]
    xg_ref[...] = (x * g_ref[...]).astype(xg_ref.dtype)
    ssq_ref[...] = _lane_partial_sq(x)


def prenorm(x, g, tm=256):
    s, d = x.shape
    return pl.pallas_call(
        _prenorm_kernel,
        grid=(s // tm,),
        in_specs=[pl.BlockSpec((tm, d), lambda i: (i, 0)),
                  pl.BlockSpec((1, d), lambda i: (0, 0))],
        out_specs=[pl.BlockSpec((tm, d), lambda i: (i, 0)),
                   pl.BlockSpec((tm, LANES), lambda i: (i, 0))],
        out_shape=[jax.ShapeDtypeStruct((s, d), BF16), jax.ShapeDtypeStruct((s, LANES), F32)],
        compiler_params=_params(("parallel",)),
        name="prenorm",
    )(x, g.reshape(1, d))


P_TM = 2048
P_TN = 256


def _resident_rows(shape):
    return pl.BlockSpec(shape, lambda i, j: (i, 0), pipeline_mode=pl.Buffered(1))


def _proj_segments():
    b0 = A_IN
    g0 = b0 + B_SHIFT_W
    c0 = g0 + B_W
    return ((0, A_IN, 0, None), (b0, B_SHIFT_W, 1, None), (g0, B_W, 2, None),
            (c0, C_W, 3, C_QSCALE), (c0 + C_W, 2 * C_W, 3, None), (c0 + 3 * C_W, C_W, 4, None))


def _inproj_kernel(a_ref, ssq_ref, w_ref, *refs):
    o_refs, r_ref = refs[:-1], refs[-1]
    j = pl.program_id(1)

    @pl.when(j == 0)
    def _():
        ms = jnp.sum(ssq_ref[...], axis=-1, keepdims=True) * (1.0 / D_MODEL)
        r_ref[...] = jnp.broadcast_to(lax.rsqrt(ms + RMS_EPS), r_ref.shape)

    for col0, width, slot, scale in _proj_segments():
        lo, hi = col0 // P_TN, (col0 + width) // P_TN

        @pl.when((j >= lo) & (j < hi))
        def _(slot=slot, scale=scale):
            r = r_ref[...]
            if scale is not None:
                r = r * scale
            acc = _dot(a_ref[...], w_ref[0].astype(BF16))
            acc = acc * jnp.concatenate([r] * (P_TN // LANES), axis=1)
            o_refs[slot][...] = acc.astype(o_refs[slot].dtype)


def inproj(xg, ssq, w_all, layer):
    m, k = xg.shape
    segs = _proj_segments()
    assert m % P_TM == 0 and all(c % P_TN == 0 and w % P_TN == 0 for c, w, _, _ in segs)
    dtypes = (F32, F32, F32, BF16, F32)
    out_specs, out_shape = [], []
    for slot, dt in enumerate(dtypes):
        lo = min(c for c, _, s, _ in segs if s == slot) // P_TN
        n = sum(w for _, w, s, _ in segs if s == slot) // P_TN
        out_specs.append(pl.BlockSpec(
            (P_TM, P_TN), lambda i, j, lo=lo, n=n: (i, jnp.clip(j - lo, 0, n - 1))))
        out_shape.append(jax.ShapeDtypeStruct((m, n * P_TN), dt))
    return pl.pallas_call(
        _inproj_kernel,
        grid=(m // P_TM, w_all.shape[2] // P_TN),
        in_specs=[_resident_rows((P_TM, k)), _resident_rows((P_TM, LANES)),
                  pl.BlockSpec((1, k, P_TN), lambda i, j: (layer, 0, j))],
        out_specs=out_specs,
        out_shape=out_shape,
        scratch_shapes=[pltpu.VMEM((P_TM, LANES), F32)],
        compiler_params=_params(("parallel", "arbitrary")),
        name="inproj",
    )(xg, ssq, w_all)


def _outproj_acc(ya_ref, yb_ref, yc_ref, wa_ref, wb_ref, wc_ref, x_ref):
    acc = _dot(ya_ref[...], wa_ref[0].astype(BF16))
    acc += _dot(yb_ref[...], wb_ref[0].astype(BF16))
    acc += _dot(yc_ref[...], wc_ref[0].astype(BF16))
    return x_ref[...] + acc


def _outproj_kernel(ya_ref, yb_ref, yc_ref, wa_ref, wb_ref, wc_ref, x_ref, o_ref):
    o_ref[...] = _outproj_acc(ya_ref, yb_ref, yc_ref, wa_ref, wb_ref, wc_ref, x_ref)


def _outproj_norm_kernel(ya_ref, yb_ref, yc_ref, wa_ref, wb_ref, wc_ref, x_ref, g_ref,
                         o_ref, xg_ref, ssq_ref):
    j = pl.program_id(1)
    xn = _outproj_acc(ya_ref, yb_ref, yc_ref, wa_ref, wb_ref, wc_ref, x_ref)
    o_ref[...] = xn
    xg_ref[...] = (xn * g_ref[...]).astype(xg_ref.dtype)
    part = _lane_partial_sq(xn)

    @pl.when(j == 0)
    def _():
        ssq_ref[...] = part

    @pl.when(j > 0)
    def _():
        ssq_ref[...] += part


def outproj(ya, yb, yc, w_all, layer, x, g_next=None):
    s = x.shape[0]
    n = w_all.shape[2]
    tm, tn = P_TM, P_TN
    tile = pl.BlockSpec((tm, tn), lambda i, j: (i, j))
    in_specs = [pl.BlockSpec((tm, A_W), lambda i, j: (i, 0)),
                pl.BlockSpec((tm, B_W), lambda i, j: (i, 0)),
                _resident_rows((tm, C_W)),
                pl.BlockSpec((1, A_W, tn), lambda i, j: (layer, 0, j)),
                pl.BlockSpec((1, B_W, tn), lambda i, j: (layer, A_W // B_W, j)),
                pl.BlockSpec((1, C_W, tn), lambda i, j: (layer, (A_W + B_W) // C_W, j)),
                tile]
    if g_next is None:
        return pl.pallas_call(
            _outproj_kernel,
            grid=(s // tm, n // tn),
            in_specs=in_specs,
            out_specs=tile,
            out_shape=jax.ShapeDtypeStruct((s, n), F32),
            compiler_params=_params(("parallel", "parallel")),
            name="outproj",
        )(ya, yb, yc, w_all, w_all, w_all, x)
    return pl.pallas_call(
        _outproj_norm_kernel,
        grid=(s // tm, n // tn),
        in_specs=in_specs + [pl.BlockSpec((1, tn), lambda i, j: (0, j))],
        out_specs=[tile, tile, pl.BlockSpec((tm, LANES), lambda i, j: (i, 0))],
        out_shape=[jax.ShapeDtypeStruct((s, n), F32), jax.ShapeDtypeStruct((s, n), BF16),
                   jax.ShapeDtypeStruct((s, LANES), F32)],
        compiler_params=_params(("parallel", "arbitrary")),
        name="outproj_norm",
    )(ya, yb, yc, w_all, w_all, w_all, x, g_next.reshape(1, n))


C_TQ = 256
C_NKB = C_BACK * CHUNK // C_TQ + 1
C_TK = C_NKB * C_TQ
C_TROW = C_TQ + C_TK


C_HPS = 8
LOG2E = math.log2(math.e)
C_QSCALE = C_DH ** -0.5 * LOG2E


def _attn_kernel(q_ref, k0_ref, k1_ref, k2_ref, v0_ref, v1_ref, v2_ref, g_ref, bias_ref,
                 o_ref):
    heads = [slice(h * C_DH, (h + 1) * C_DH) for h in range(C_HPS)]
    s = [_dot_nt(q_ref[:, c],
                 jnp.concatenate([k0_ref[:, c], k1_ref[:, c], k2_ref[:, c]], axis=0))
         + bias_ref[h] for h, c in enumerate(heads)]
    p = [jnp.exp2(x - jnp.max(x, axis=-1, keepdims=True)) for x in s]
    l = [jnp.sum(x, axis=-1, keepdims=True) for x in p]
    o = [_dot(x.astype(BF16),
              jnp.concatenate([v0_ref[:, c], v1_ref[:, c], v2_ref[:, c]], axis=0))
         for x, c in zip(p, heads)]
    for h, c in enumerate(heads):
        o_ref[:, c] = (o[h] / l[h] * _silu(g_ref[:, c].astype(F32))).astype(o_ref.dtype)


def _bias_kernel(r_ref, o_ref):
    x = jnp.broadcast_to(r_ref[0], (C_TQ, C_TROW))
    y = pltpu.roll(x, 0, 1, stride=1, stride_axis=0)[:, :C_TK]
    qc = lax.broadcasted_iota(jnp.int32, (C_TQ, C_TK), 0) // CHUNK
    kcol = lax.broadcasted_iota(jnp.int32, (C_TQ, C_TK), 1)
    dchunk = qc - (kcol // CHUNK - (C_TK - C_TQ) // CHUNK)
    band = (dchunk >= 0) & (dchunk <= C_BACK)
    for v in range(C_NKB):
        o_ref[v, 0] = jnp.where(band & (kcol >= (C_NKB - 1 - v) * C_TQ), y, NEG)


def _attn_bias_tables(rel_bias):
    j = np.arange(C_TROW)
    dist = (C_TK - j) % C_TROW - C_TQ
    idx = np.clip(dist, -(CHUNK - 1), C_REL_CLIP) + (CHUNK - 1)
    nl, nh = rel_bias.shape[0], rel_bias.shape[1]
    row = jnp.take(rel_bias, jnp.asarray(idx, jnp.int32), axis=-1) * LOG2E
    return pl.pallas_call(
        _bias_kernel,
        grid=(nl, nh),
        in_specs=[pl.BlockSpec((1, 1, C_TROW), lambda l, h: (l * nh + h, 0, 0))],
        out_specs=pl.BlockSpec((C_NKB, 1, C_TQ, C_TK), lambda l, h: (l, h, 0, 0)),
        out_shape=jax.ShapeDtypeStruct((nl * C_NKB, nh, C_TQ, C_TK), F32),
        compiler_params=_params(("parallel", "parallel")),
        name="attn_bias",
    )(row.reshape(nl * nh, 1, C_TROW))


def chunk_attention(qkv, gate, bias_tbl, layer, out_dtype):
    s = qkv.shape[0]
    nq = s // C_TQ
    wblk = C_HPS * C_DH
    ng = C_HEADS // C_HPS

    def kv_spec(base, back):
        return pl.BlockSpec((C_TQ, wblk),
                            lambda h, i: (jnp.maximum(i - back, 0), base + h))

    def bias_map(h, i):
        return (layer * C_NKB + jnp.minimum(i, C_NKB - 1), h, 0, 0)

    return pl.pallas_call(
        _attn_kernel,
        grid=(ng, nq),
        in_specs=[pl.BlockSpec((C_TQ, wblk), lambda h, i: (i, h)),
                  kv_spec(ng, 2), kv_spec(ng, 1), kv_spec(ng, 0),
                  kv_spec(2 * ng, 2), kv_spec(2 * ng, 1), kv_spec(2 * ng, 0),
                  pl.BlockSpec((C_TQ, wblk), lambda h, i: (i, h)),
                  pl.BlockSpec((None, C_HPS, C_TQ, C_TK), bias_map)],
        out_specs=pl.BlockSpec((C_TQ, wblk), lambda h, i: (i, h)),
        out_shape=jax.ShapeDtypeStruct((s, C_W), out_dtype),
        compiler_params=_params(("parallel", "parallel")),
        name="chunk_attn",
    )(qkv, qkv, qkv, qkv, qkv, qkv, qkv, gate, bias_tbl)


A_TB = 256


def _swap_pairs(x):
    lane = lax.broadcasted_iota(jnp.int32, x.shape, x.ndim - 1)
    nxt = pltpu.roll(x, LANES - 1, x.ndim - 1)
    prv = pltpu.roll(x, 1, x.ndim - 1)
    return jnp.where(lane % 2 == 0, nxt, prv)


def _retention_kernel(q_ref, k_ref, v_ref, g_ref, cos_ref, sin_ref, mask_ref, dq_ref,
                      dk_ref, gb_ref, o_ref, st_ref):
    i = pl.program_id(0)

    @pl.when(i == 0)
    def _():
        st_ref[...] = jnp.zeros_like(st_ref)

    cos = cos_ref[...]
    sin = sin_ref[...]
    hs = range(A_HEADS)

    def rot(ref, h):
        x = ref[:, h * A_DK:(h + 1) * A_DK].astype(F32)
        return x * cos + _swap_pairs(x) * sin

    q = [rot(q_ref, h) for h in hs]
    k = [rot(k_ref, h) * (A_DK ** -0.5) for h in hs]
    v = [v_ref[:, h * A_DV:(h + 1) * A_DV].astype(BF16) for h in hs]
    st = [st_ref[h] for h in hs]
    sc = [_dot_nt(q[h].astype(BF16), k[h].astype(BF16)) * mask_ref[h] for h in hs]
    oi = [_dot((q[h] * dq_ref[h]).astype(BF16), st[h].astype(BF16)) for h in hs]
    o = [_dot(sc[h].astype(BF16), v[h]) + oi[h] for h in hs]
    kv = [_dot_tn((k[h] * dk_ref[h]).astype(BF16), v[h]) for h in hs]
    for h in hs:
        st_ref[h] = gb_ref[h] * st[h] + kv[h]
        on = o[h] * lax.rsqrt(jnp.mean(o[h] * o[h], axis=-1, keepdims=True) + A_NORM_EPS)
        cols = slice(h * A_DV, (h + 1) * A_DV)
        o_ref[:, cols] = (on * _silu(g_ref[:, cols].astype(F32))).astype(o_ref.dtype)


def _retention_tables():
    h = np.arange(A_HEADS, dtype=np.float64)
    log_g = np.log(1.0 - 2.0 ** (-5.0 - h))
    pos = np.arange(A_TB)
    d = pos[:, None] - pos[None, :]
    cq = pos[:, None] // CHUNK
    ck = pos[None, :] // CHUNK
    same = cq == ck
    past = ck < cq
    expo = np.where(same, np.abs(d), np.where(past, d, 0)).astype(np.float64)
    mask = np.exp(log_g[:, None, None] * expo[None]) * (same | past)[None]
    dq = np.exp(log_g[:, None] * (pos + 1.0)[None, :])
    dk = np.exp(log_g[:, None] * (A_TB - 1.0 - pos)[None, :])
    gb = np.exp(log_g * A_TB)
    ones = np.ones((1, 1, LANES))
    return (jnp.asarray(mask, F32), jnp.asarray(dq[:, :, None] * ones, F32),
            jnp.asarray(dk[:, :, None] * ones, F32),
            jnp.asarray(gb[:, None, None] * np.ones((1, 1, A_DV)), F32))


def _rotary_tables(s):
    inv = 1.0 / (ROPE_BASE ** (jnp.arange(0, A_DK, 2, dtype=F32) / A_DK))
    ang = jnp.arange(s).astype(F32)[:, None] * inv[None, :]
    cos = jnp.repeat(jnp.cos(ang), 2, axis=-1)
    sin = jnp.repeat(jnp.sin(ang), 2, axis=-1)
    sign = jnp.asarray(np.tile(np.array([-1.0, 1.0]), A_DK // 2), F32)
    return cos, sin * sign[None, :]


def retention(ha, rot_tbl, ret_tbl, out_dtype):
    s = ha.shape[0]
    cos, sin = rot_tbl
    mask, dq, dk, gb = ret_tbl
    whole = lambda shape: pl.BlockSpec(shape, lambda i: (0,) * len(shape))
    return pl.pallas_call(
        _retention_kernel,
        grid=(s // A_TB,),
        in_specs=[pl.BlockSpec((A_TB, A_QK), lambda i: (i, 0)),
                  pl.BlockSpec((A_TB, A_QK), lambda i: (i, 1)),
                  pl.BlockSpec((A_TB, A_W), lambda i: (i, 2 * A_QK // A_W)),
                  pl.BlockSpec((A_TB, A_W), lambda i: (i, 2 * A_QK // A_W + 1)),
                  pl.BlockSpec((A_TB, A_DK), lambda i: (i, 0)),
                  pl.BlockSpec((A_TB, A_DK), lambda i: (i, 0)),
                  whole((A_HEADS, A_TB, A_TB)), whole((A_HEADS, A_TB, LANES)),
                  whole((A_HEADS, A_TB, LANES)), whole((A_HEADS, 1, A_DV))],
        out_specs=pl.BlockSpec((A_TB, A_W), lambda i: (i, 0)),
        out_shape=jax.ShapeDtypeStruct((s, A_W), out_dtype),
        scratch_shapes=[pltpu.VMEM((A_HEADS, A_DK, A_DV), F32)],
        compiler_params=_params(("arbitrary",)),
        name="retention",
    )(ha, ha, ha, ha, cos, sin, mask, dq, dk, gb)


B_TP = 512
B_TC = 512
B_NCB = B_TC // CHUNK
B_NPAIR = B_W // LANES


def _seg_ones():
    r = lax.broadcasted_iota(jnp.int32, (LANES, LANES), 0) // B_HEAD
    c = lax.broadcasted_iota(jnp.int32, (LANES, LANES), 1) // B_HEAD
    return (r == c).astype(BF16)


def _seg_sum(x):
    ones = _seg_ones()
    hi = x.astype(BF16)
    lo = (x - hi.astype(F32)).astype(BF16)
    parts = [_dot(hi[:, j:j + LANES], ones) + _dot(lo[:, j:j + LANES], ones)
             for j in range(0, x.shape[1], LANES)]
    return jnp.concatenate(parts, axis=1)


def _rwkv_prep_kernel(hb_ref, prev_ref, mu_ref, w0_ref, wup_ref, a0_ref, aup_ref, kk_ref,
                      ka_ref, rk_ref, rt_ref, kq_ref, bt_ref, kt_ref, v_ref, bte_ref,
                      kte_ref, bonus_ref, pl_ref):
    i = pl.program_id(0)
    hb = hb_ref[...]
    rows = lax.broadcasted_iota(jnp.int32, hb.shape, 0)
    prev_row = jnp.where(i == 0, 0.0, prev_ref[7:8, :])
    shifted = jnp.where(rows == 0, prev_row, pltpu.roll(hb, 1, 0))
    xs = hb + (shifted - hb) * mu_ref[...]
    r = xs[:, 0:B_W]
    k = xs[:, B_W:2 * B_W]
    v = xs[:, 2 * B_W:3 * B_W]
    wd = xs[:, 3 * B_W:3 * B_W + B_RANK]
    ad = xs[:, 3 * B_W + B_RANK:3 * B_W + 2 * B_RANK]
    z = w0_ref[...] + _mm(jnp.tanh(wd), wup_ref[...])
    lw = -math.exp(-0.5) * jax.nn.sigmoid(z)
    a = jax.nn.sigmoid(a0_ref[...] + _mm(ad, aup_ref[...]))
    kk = k * kk_ref[...]
    nrm = jnp.maximum(jnp.sqrt(_seg_sum(kk * kk)), 1e-12)
    kk = kk / nrm
    kmod = k * (1.0 + (a - 1.0) * ka_ref[...])
    bonus_ref[...] = _seg_sum(r * kmod * rk_ref[...]) * v
    tr = lax.broadcasted_iota(jnp.int32, (B_TP, B_TP), 0)
    tc = lax.broadcasted_iota(jnp.int32, (B_TP, B_TP), 1)
    tri = ((tr // CHUNK == tc // CHUNK) & (tr >= tc)).astype(BF16)
    hi = lw.astype(BF16)
    rest = lw - hi.astype(F32)
    mid = rest.astype(BF16)
    lo = (rest - mid.astype(F32)).astype(BF16)
    cum = _dot(tri, hi) + _dot(tri, mid) + _dot(tri, lo)
    nc = B_TP // CHUNK
    ends = [cum[(c + 1) * CHUNK - 1:(c + 1) * CHUNK, :] for c in range(nc)]
    tot = jnp.concatenate([jnp.broadcast_to(e, (CHUNK, B_W)) for e in ends], axis=0)
    p_inv = jnp.exp(-cum)
    p_end = jnp.exp(tot - cum)
    rt_ref[...] = (r * jnp.exp(cum)).astype(BF16)
    kq_ref[...] = (kk * jnp.exp(cum - lw)).astype(BF16)
    bt_ref[...] = (kk * a * p_inv).astype(BF16)
    kt_ref[...] = (kmod * p_inv).astype(BF16)
    v_ref[...] = v.astype(BF16)
    bte_ref[...] = (kk * a * p_end).astype(BF16)
    kte_ref[...] = (kmod * p_end).astype(BF16)
    pl_ref[...] = jnp.exp(jnp.concatenate(ends, axis=0))


def rwkv_prep(hb, mu, w0, w_up, a0, a_up, k_k, k_a, r_k):
    s = hb.shape[0]
    nb = s // B_TP
    vec = lambda n: pl.BlockSpec((1, n), lambda i: (0, 0))
    full = lambda a, b: pl.BlockSpec((a, b), lambda i: (0, 0))
    big = pl.BlockSpec((B_TP, B_W), lambda i: (i, 0))
    outs = ([jax.ShapeDtypeStruct((s, B_W), BF16)] * 7
            + [jax.ShapeDtypeStruct((s, B_W), F32),
               jax.ShapeDtypeStruct((s // CHUNK, B_W), F32)])
    return pl.pallas_call(
        _rwkv_prep_kernel,
        grid=(nb,),
        in_specs=[pl.BlockSpec((B_TP, B_SHIFT_W), lambda i: (i, 0)),
                  pl.BlockSpec((8, B_SHIFT_W),
                               lambda i: (jnp.maximum(i * (B_TP // 8) - 1, 0), 0)),
                  vec(B_SHIFT_W), vec(B_W), full(B_RANK, B_W), vec(B_W), full(B_RANK, B_W),
                  vec(B_W), vec(B_W), vec(B_W)],
        out_specs=[big] * 8 + [pl.BlockSpec((B_TP // CHUNK, B_W), lambda i: (i, 0))],
        out_shape=outs,
        compiler_params=_params(("parallel",)),
        name="rwkv_prep",
    )(hb, hb, mu.reshape(1, -1), w0.reshape(1, -1), w_up, a0.reshape(1, -1), a_up,
      k_k.reshape(1, -1), k_a.reshape(1, -1), r_k.reshape(1, -1))


def _stack_heads(x):
    lane = lax.broadcasted_iota(jnp.int32, x.shape, 1)
    zero = jnp.zeros_like(x)
    return jnp.concatenate([jnp.where(lane < B_HEAD, x, zero),
                            jnp.where(lane >= B_HEAD, x, zero)], axis=0)


def _mm(a, b):
    return _dot(a.astype(BF16), b.astype(BF16))


def _rwkv_intra_kernel(rt_ref, kq_ref, bt_ref, kt_ref, v_ref, bte_ref, kte_ref, pl_ref,
                       mz_ref, gz_ref, rq_ref, yc_ref):
    n2 = 2 * CHUNK
    r_i = lax.broadcasted_iota(jnp.int32, (n2, n2), 0)
    c_i = lax.broadcasted_iota(jnp.int32, (n2, n2), 1)
    same_head = (r_i // CHUNK) == (c_i // CHUNK)
    strict = (same_head & (r_i > c_i)).astype(F32)
    incl = (same_head & (r_i >= c_i)).astype(F32)
    eye = (r_i == c_i).astype(F32)
    bd16 = ((r_i // 16) == (c_i // 16)).astype(F32)
    bd32 = ((r_i // 32) == (c_i // 32)).astype(F32)

    cs = range(B_NCB)

    def load(ref):
        return [_stack_heads(ref[c * CHUNK:(c + 1) * CHUNK, :]) for c in cs]

    def each(fn, *lists):
        return [fn(*args) for args in zip(*lists)]

    rt, kq, bt, kt, vv, bte, kte = (load(r) for r in (rt_ref, kq_ref, bt_ref, kt_ref, v_ref,
                                                      bte_ref, kte_ref))
    ab = each(lambda kq_, rt_, bt_, kt_: _dot_nt(jnp.concatenate([kq_, rt_], axis=0),
                                                 jnp.concatenate([bt_, kt_], axis=0)),
              kq, rt, bt, kt)
    a_m = [x[:n2, :n2] * strict for x in ab]
    b_m = [x[:n2, n2:] * strict for x in ab]
    aq = [x[n2:, :n2] * incl for x in ab]
    bq = [x[n2:, n2:] * incl for x in ab]
    nn = [-(x * bd16) for x in a_m]
    t = [eye + x for x in nn]
    sq = each(_mm, nn, nn)
    for step in range(3):
        t = each(lambda t_, s_: t_ + _mm(t_, s_), t, sq)
        if step < 2:
            sq = each(_mm, sq, sq)
    for blk in (bd32 - bd16, 1.0 - bd32):
        e = each(lambda a_, t_: _mm(a_ * blk, t_), a_m, t)
        t = each(lambda t_, e_: t_ - _mm(t_, e_), t, e)
    bv = each(_mm, b_m, vv)
    tw = each(lambda t_, kq_, bv_: _mm(t_, jnp.concatenate([kq_.astype(F32), bv_], axis=1)),
              t, kq, bv)
    w = [x[:, :n2].astype(BF16) for x in tw]
    u0 = [(-x[:, n2:]).astype(BF16) for x in tw]
    aw = each(lambda aq_, w_, u_: _mm(aq_, jnp.concatenate([w_, u_], axis=1)), aq, w, u0)
    bqv = each(_mm, bq, vv)
    wb = each(_dot_tn, w, bte)
    gz = each(lambda u_, v_, b_, k_: _dot_tn(jnp.concatenate([u_, v_], axis=0),
                                             jnp.concatenate([b_, k_], axis=0)),
              u0, vv, bte, kte)
    for c in cs:
        rq_ref[0, c] = (rt[c].astype(F32) - aw[c][:, :n2]).astype(BF16)
        yc2 = bqv[c] + aw[c][:, n2:]
        yc_ref[c * CHUNK:(c + 1) * CHUNK, :] = yc2[:CHUNK] + yc2[CHUNK:]
        mz_ref[0, c] = (eye * pl_ref[c:c + 1, :] - wb[c]).astype(BF16)
        gz_ref[0, c] = gz[c]


def rwkv_intra(prep):
    rt, kq, bt, kt, v, bte, kte, _, p_end = prep
    s = rt.shape[0]
    nchunk = s // CHUNK
    blk = pl.BlockSpec((B_TC, LANES), lambda p, i: (i, p))
    mat = pl.BlockSpec((1, B_NCB, LANES, LANES), lambda p, i: (p, i, 0, 0))
    mshape = (B_NPAIR, nchunk, LANES, LANES)
    return pl.pallas_call(
        _rwkv_intra_kernel,
        grid=(B_NPAIR, s // B_TC),
        in_specs=[blk] * 7 + [pl.BlockSpec((B_NCB, LANES), lambda p, i: (i, p))],
        out_specs=[mat, mat, mat, blk],
        out_shape=[jax.ShapeDtypeStruct(mshape, BF16), jax.ShapeDtypeStruct(mshape, F32),
                   jax.ShapeDtypeStruct(mshape, BF16), jax.ShapeDtypeStruct((s, B_W), F32)],
        compiler_params=_params(("parallel", "parallel")),
        name="rwkv_intra",
    )(rt, kq, bt, kt, v, bte, kte, p_end)


def _rwkv_serial_kernel(mz_ref, gz_ref, rq_ref, yc_ref, bonus_ref, g_ref, lng_ref, lnb_ref,
                        o_ref, zt_ref, y_ref):
    i = pl.program_id(0)

    @pl.when(i == 0)
    def _():
        zt_ref[...] = jnp.zeros_like(zt_ref)

    def chunk(c, carry):
        rows = pl.ds(pl.multiple_of(c * CHUNK, CHUNK), CHUNK)
        for p in range(B_NPAIR):
            cols = slice(p * LANES, (p + 1) * LANES)
            zt = zt_ref[p].astype(BF16)
            y2 = _dot_nt(rq_ref[p, c], zt)
            zt_ref[p] = _dot(zt, mz_ref[p, c]) + gz_ref[p, c]
            y_ref[rows, cols] = y2[:CHUNK] + y2[CHUNK:] + yc_ref[rows, cols]
        return carry

    lax.fori_loop(0, B_NCB, chunk, 0)
    y = y_ref[...]
    mean = _seg_sum(y) * (1.0 / B_HEAD)
    yc = y - mean
    var = _seg_sum(yc * yc) * (1.0 / B_HEAD)
    yn = yc * lax.rsqrt(var + B_LN_EPS) * lng_ref[...] + lnb_ref[...]
    o_ref[...] = ((yn + bonus_ref[...]) * _silu(g_ref[...].astype(F32))).astype(o_ref.dtype)


def rwkv_serial(intra, bonus, gate, ln_g, ln_b, out_dtype):
    mz, gz, rq, yc = intra
    s = yc.shape[0]
    mat = pl.BlockSpec((B_NPAIR, B_NCB, LANES, LANES), lambda i: (0, i, 0, 0))
    blk = pl.BlockSpec((B_TC, B_W), lambda i: (i, 0))
    vec = pl.BlockSpec((1, B_W), lambda i: (0, 0))
    return pl.pallas_call(
        _rwkv_serial_kernel,
        grid=(s // B_TC,),
        in_specs=[mat, mat, mat, blk, blk, blk, vec, vec],
        out_specs=blk,
        out_shape=jax.ShapeDtypeStruct((s, B_W), out_dtype),
        scratch_shapes=[pltpu.VMEM((B_NPAIR, LANES, LANES), F32),
                        pltpu.VMEM((B_TC, B_W), F32)],
        compiler_params=_params(("arbitrary",)),
        name="rwkv_serial",
    )(mz, gz, rq, yc, bonus, gate, ln_g.reshape(1, -1), ln_b.reshape(1, -1))


def rwkv7(hb, gate, mu, w0, w_up, a0, a_up, k_k, k_a, r_k, ln_g, ln_b, out_dtype):
    prep = rwkv_prep(hb, mu, w0, w_up, a0, a_up, k_k, k_a, r_k)
    intra = rwkv_intra(prep)
    return rwkv_serial(intra, prep[7], gate, ln_g, ln_b, out_dtype)


def kernel(x, norm_g, w_in, w_out, b_mu, b_w0, b_w_up, b_a0, b_a_up, b_k_k, b_k_a, b_r_k,
           b_ln_g, b_ln_b, c_rel_bias, final_g):
    bsz, s, d = x.shape
    assert bsz == 1 and d == D_MODEL
    xs = x.reshape(s, d)
    rot_tbl = _rotary_tables(s)
    ret_tbl = _retention_tables()
    bias_tbl = _attn_bias_tables(c_rel_bias)
    xg, ssq = prenorm(xs, norm_g[0])
    for l in range(DEPTH):
        ha, hb, gb, qkv, gc = inproj(xg, ssq, w_in, l)
        ya = retention(ha, rot_tbl, ret_tbl, BF16)
        yb = rwkv7(hb, gb, b_mu[l], b_w0[l], b_w_up[l], b_a0[l], b_a_up[l], b_k_k[l],
                   b_k_a[l], b_r_k[l].reshape(-1), b_ln_g[l], b_ln_b[l], BF16)
        yc = chunk_attention(qkv, gc, bias_tbl, l, BF16)
        if l + 1 < DEPTH:
            xs, xg, ssq = outproj(ya, yb, yc, w_out, l, xs, norm_g[l + 1])
        else:
            xs = outproj(ya, yb, yc, w_out, l, xs)
    return rmsnorm(xs, final_g, F32).reshape(bsz, s, d)
```

```python
import math

import numpy as np
import jax
import jax.numpy as jnp
from jax import lax
from jax.experimental import pallas as pl
from jax.experimental.pallas import tpu as pltpu

F32 = jnp.float32
BF16 = jnp.bfloat16

D_MODEL = 4096
DEPTH = 4
CHUNK = 64
A_W = 1024
A_HEADS = 4
A_DV = 256
A_DK = 128
A_QK = 512
A_IN = 2 * A_QK + 2 * A_W
ROPE_BASE = 10000.0
A_NORM_EPS = 1e-6
B_W = 1024
B_HEAD = 64
B_HEADS = 16
B_RANK = 128
B_SHIFT_W = 3 * B_W + 2 * B_RANK
B_LN_EPS = 64e-5
B_IN = B_SHIFT_W + B_W
C_W = 2048
C_HEADS = 16
C_DH = 128
C_BACK = 8
C_REL_CLIP = 128
RMS_EPS = 1e-6

LANES = 128
VMEM_LIMIT = 56 * 1024 * 1024

NEG = -1e30


def _params(sem):
    return pltpu.CompilerParams(dimension_semantics=sem, vmem_limit_bytes=VMEM_LIMIT)


def _dot(a, b, prec=None):
    return jnp.dot(a, b, preferred_element_type=F32, precision=prec)


def _dot_nt(a, b, prec=None):
    return lax.dot_general(a, b, (((1,), (1,)), ((), ())), preferred_element_type=F32,
                           precision=prec)


def _dot_tn(a, b, prec=None):
    return lax.dot_general(a, b, (((0,), (0,)), ((), ())), preferred_element_type=F32,
                           precision=prec)


def _silu(g):
    return g * jax.nn.sigmoid(g)


def _rmsnorm_kernel(x_ref, g_ref, o_ref):
    x = x_ref[...]
    ms = jnp.mean(x * x, axis=-1, keepdims=True)
    o_ref[...] = (x * lax.rsqrt(ms + RMS_EPS) * g_ref[...]).astype(o_ref.dtype)


def rmsnorm(x, g, out_dtype, tm=256):
    s, d = x.shape
    return pl.pallas_call(
        _rmsnorm_kernel,
        grid=(s // tm,),
        in_specs=[pl.BlockSpec((tm, d), lambda i: (i, 0)),
                  pl.BlockSpec((1, d), lambda i: (0, 0))],
        out_specs=pl.BlockSpec((tm, d), lambda i: (i, 0)),
        out_shape=jax.ShapeDtypeStruct((s, d), out_dtype),
        compiler_params=_params(("parallel",)),
        name="rmsnorm",
    )(x, g.reshape(1, d))


def _lane_partial_sq(x):
    sq = x * x
    acc = sq[:, 0:LANES]
    for j in range(LANES, x.shape[1], LANES):
        acc = acc + sq[:, j:j + LANES]
    return acc


def _prenorm_kernel(x_ref, g_ref, xg_ref, ssq_ref):
    x = x_ref[...]
    xg_ref[...] = (x * g_ref[...]).astype(xg_ref.dtype)
    ssq_ref[...] = _lane_partial_sq(x)


def prenorm(x, g, tm=256):
    s, d = x.shape
    return pl.pallas_call(
        _prenorm_kernel,
        grid=(s // tm,),
        in_specs=[pl.BlockSpec((tm, d), lambda i: (i, 0)),
                  pl.BlockSpec((1, d), lambda i: (0, 0))],
        out_specs=[pl.BlockSpec((tm, d), lambda i: (i, 0)),
                   pl.BlockSpec((tm, LANES), lambda i: (i, 0))],
        out_shape=[jax.ShapeDtypeStruct((s, d), BF16), jax.ShapeDtypeStruct((s, LANES), F32)],
        compiler_params=_params(("parallel",)),
        name="prenorm",
    )(x, g.reshape(1, d))


P_TM = 2048
P_TN = 256


def _resident_rows(shape):
    return pl.BlockSpec(shape, lambda i, j: (i, 0), pipeline_mode=pl.Buffered(1))


def _proj_segments():
    b0 = A_IN
    g0 = b0 + B_SHIFT_W
    c0 = g0 + B_W
    return ((0, A_IN, 0, None), (b0, B_SHIFT_W, 1, None), (g0, B_W, 2, None),
            (c0, C_W, 3, C_QSCALE), (c0 + C_W, 2 * C_W, 3, None), (c0 + 3 * C_W, C_W, 4, None))


def _inproj_kernel(a_ref, ssq_ref, w_ref, *refs):
    o_refs, r_ref = refs[:-1], refs[-1]
    j = pl.program_id(1)

    @pl.when(j == 0)
    def _():
        ms = jnp.sum(ssq_ref[...], axis=-1, keepdims=True) * (1.0 / D_MODEL)
        r_ref[...] = jnp.broadcast_to(lax.rsqrt(ms + RMS_EPS), r_ref.shape)

    for col0, width, slot, scale in _proj_segments():
        lo, hi = col0 // P_TN, (col0 + width) // P_TN

        @pl.when((j >= lo) & (j < hi))
        def _(slot=slot, scale=scale):
            r = r_ref[...]
            if scale is not None:
                r = r * scale
            acc = _dot(a_ref[...], w_ref[0].astype(BF16))
            acc = acc * jnp.concatenate([r] * (P_TN // LANES), axis=1)
            o_refs[slot][...] = acc.astype(o_refs[slot].dtype)


def inproj(xg, ssq, w_all, layer):
    m, k = xg.shape
    segs = _proj_segments()
    assert m % P_TM == 0 and all(c % P_TN == 0 and w % P_TN == 0 for c, w, _, _ in segs)
    dtypes = (F32, F32, F32, BF16, F32)
    out_specs, out_shape = [], []
    for slot, dt in enumerate(dtypes):
        lo = min(c for c, _, s, _ in segs if s == slot) // P_TN
        n = sum(w for _, w, s, _ in segs if s == slot) // P_TN
        out_specs.append(pl.BlockSpec(
            (P_TM, P_TN), lambda i, j, lo=lo, n=n: (i, jnp.clip(j - lo, 0, n - 1))))
        out_shape.append(jax.ShapeDtypeStruct((m, n * P_TN), dt))
    return pl.pallas_call(
        _inproj_kernel,
        grid=(m // P_TM, w_all.shape[2] // P_TN),
        in_specs=[_resident_rows((P_TM, k)), _resident_rows((P_TM, LANES)),
                  pl.BlockSpec((1, k, P_TN), lambda i, j: (layer, 0, j))],
        out_specs=out_specs,
        out_shape=out_shape,
        scratch_shapes=[pltpu.VMEM((P_TM, LANES), F32)],
        compiler_params=_params(("parallel", "arbitrary")),
        name="inproj",
    )(xg, ssq, w_all)


def _outproj_acc(ya_ref, yb_ref, yc_ref, wa_ref, wb_ref, wc_ref, x_ref):
    acc = _dot(ya_ref[...], wa_ref[0].astype(BF16))
    acc += _dot(yb_ref[...], wb_ref[0].astype(BF16))
    acc += _dot(yc_ref[...], wc_ref[0].astype(BF16))
    return x_ref[...] + acc


def _outproj_kernel(ya_ref, yb_ref, yc_ref, wa_ref, wb_ref, wc_ref, x_ref, o_ref):
    o_ref[...] = _outproj_acc(ya_ref, yb_ref, yc_ref, wa_ref, wb_ref, wc_ref, x_ref)


def _outproj_norm_kernel(ya_ref, yb_ref, yc_ref, wa_ref, wb_ref, wc_ref, x_ref, g_ref,
                         o_ref, xg_ref, ssq_ref):
    j = pl.program_id(1)
    xn = _outproj_acc(ya_ref, yb_ref, yc_ref, wa_ref, wb_ref, wc_ref, x_ref)
    o_ref[...] = xn
    xg_ref[...] = (xn * g_ref[...]).astype(xg_ref.dtype)
    part = _lane_partial_sq(xn)

    @pl.when(j == 0)
    def _():
        ssq_ref[...] = part

    @pl.when(j > 0)
    def _():
        ssq_ref[...] += part


def outproj(ya, yb, yc, w_all, layer, x, g_next=None):
    s = x.shape[0]
    n = w_all.shape[2]
    tm, tn = P_TM, P_TN
    tile = pl.BlockSpec((tm, tn), lambda i, j: (i, j))
    in_specs = [pl.BlockSpec((tm, A_W), lambda i, j: (i, 0)),
                pl.BlockSpec((tm, B_W), lambda i, j: (i, 0)),
                _resident_rows((tm, C_W)),
                pl.BlockSpec((1, A_W, tn), lambda i, j: (layer, 0, j)),
                pl.BlockSpec((1, B_W, tn), lambda i, j: (layer, A_W // B_W, j)),
                pl.BlockSpec((1, C_W, tn), lambda i, j: (layer, (A_W + B_W) // C_W, j)),
                tile]
    if g_next is None:
        return pl.pallas_call(
            _outproj_kernel,
            grid=(s // tm, n // tn),
            in_specs=in_specs,
            out_specs=tile,
            out_shape=jax.ShapeDtypeStruct((s, n), F32),
            compiler_params=_params(("parallel", "parallel")),
            name="outproj",
        )(ya, yb, yc, w_all, w_all, w_all, x)
    return pl.pallas_call(
        _outproj_norm_kernel,
        grid=(s // tm, n // tn),
        in_specs=in_specs + [pl.BlockSpec((1, tn), lambda i, j: (0, j))],
        out_specs=[tile, tile, pl.BlockSpec((tm, LANES), lambda i, j: (i, 0))],
        out_shape=[jax.ShapeDtypeStruct((s, n), F32), jax.ShapeDtypeStruct((s, n), BF16),
                   jax.ShapeDtypeStruct((s, LANES), F32)],
        compiler_params=_params(("parallel", "arbitrary")),
        name="outproj_norm",
    )(ya, yb, yc, w_all, w_all, w_all, x, g_next.reshape(1, n))


C_TQ = 256
C_NKB = C_BACK * CHUNK // C_TQ + 1
C_TK = C_NKB * C_TQ
C_TROW = C_TQ + C_TK


C_HPS = 8
LOG2E = math.log2(math.e)
C_QSCALE = C_DH ** -0.5 * LOG2E


def _attn_kernel(q_ref, k0_ref, k1_ref, k2_ref, v0_ref, v1_ref, v2_ref, g_ref, bias_ref,
                 o_ref):
    heads = [slice(h * C_DH, (h + 1) * C_DH) for h in range(C_HPS)]

    def scores(h):
        c = heads[h]
        kw = jnp.concatenate([k0_ref[:, c], k1_ref[:, c], k2_ref[:, c]], axis=0)
        return _dot_nt(q_ref[:, c], kw) + bias_ref[h]

    s = scores(0)
    for h, c in enumerate(heads):
        s_next = scores(h + 1) if h + 1 < C_HPS else None
        p = jnp.exp2(s - jnp.max(s, axis=-1, keepdims=True))
        l = jnp.sum(p, axis=-1, keepdims=True)
        vw = jnp.concatenate([v0_ref[:, c], v1_ref[:, c], v2_ref[:, c]], axis=0)
        o = _dot(p.astype(BF16), vw)
        o_ref[:, c] = (o / l * _silu(g_ref[:, c].astype(F32))).astype(o_ref.dtype)
        s = s_next


def _bias_kernel(r_ref, o_ref):
    x = jnp.broadcast_to(r_ref[0], (C_TQ, C_TROW))
    y = pltpu.roll(x, 0, 1, stride=1, stride_axis=0)[:, :C_TK]
    qc = lax.broadcasted_iota(jnp.int32, (C_TQ, C_TK), 0) // CHUNK
    kcol = lax.broadcasted_iota(jnp.int32, (C_TQ, C_TK), 1)
    dchunk = qc - (kcol // CHUNK - (C_TK - C_TQ) // CHUNK)
    band = (dchunk >= 0) & (dchunk <= C_BACK)
    for v in range(C_NKB):
        o_ref[v, 0] = jnp.where(band & (kcol >= (C_NKB - 1 - v) * C_TQ), y, NEG)


def _attn_bias_tables(rel_bias):
    j = np.arange(C_TROW)
    dist = (C_TK - j) % C_TROW - C_TQ
    idx = np.clip(dist, -(CHUNK - 1), C_REL_CLIP) + (CHUNK - 1)
    nl, nh = rel_bias.shape[0], rel_bias.shape[1]
    row = jnp.take(rel_bias, jnp.asarray(idx, jnp.int32), axis=-1) * LOG2E
    return pl.pallas_call(
        _bias_kernel,
        grid=(nl, nh),
        in_specs=[pl.BlockSpec((1, 1, C_TROW), lambda l, h: (l * nh + h, 0, 0))],
        out_specs=pl.BlockSpec((C_NKB, 1, C_TQ, C_TK), lambda l, h: (l, h, 0, 0)),
        out_shape=jax.ShapeDtypeStruct((nl * C_NKB, nh, C_TQ, C_TK), F32),
        compiler_params=_params(("parallel", "parallel")),
        name="attn_bias",
    )(row.reshape(nl * nh, 1, C_TROW))


def chunk_attention(qkv, gate, bias_tbl, layer, out_dtype):
    s = qkv.shape[0]
    nq = s // C_TQ
    wblk = C_HPS * C_DH
    ng = C_HEADS // C_HPS

    def kv_spec(base, back):
        return pl.BlockSpec((C_TQ, wblk),
                            lambda h, i: (jnp.maximum(i - back, 0), base + h))

    def bias_map(h, i):
        return (layer * C_NKB + jnp.minimum(i, C_NKB - 1), h, 0, 0)

    return pl.pallas_call(
        _attn_kernel,
        grid=(ng, nq),
        in_specs=[pl.BlockSpec((C_TQ, wblk), lambda h, i: (i, h)),
                  kv_spec(ng, 2), kv_spec(ng, 1), kv_spec(ng, 0),
                  kv_spec(2 * ng, 2), kv_spec(2 * ng, 1), kv_spec(2 * ng, 0),
                  pl.BlockSpec((C_TQ, wblk), lambda h, i: (i, h)),
                  pl.BlockSpec((None, C_HPS, C_TQ, C_TK), bias_map)],
        out_specs=pl.BlockSpec((C_TQ, wblk), lambda h, i: (i, h)),
        out_shape=jax.ShapeDtypeStruct((s, C_W), out_dtype),
        compiler_params=_params(("parallel", "parallel")),
        name="chunk_attn",
    )(qkv, qkv, qkv, qkv, qkv, qkv, qkv, gate, bias_tbl)


A_TB = 256


def _swap_pairs(x):
    lane = lax.broadcasted_iota(jnp.int32, x.shape, x.ndim - 1)
    nxt = pltpu.roll(x, LANES - 1, x.ndim - 1)
    prv = pltpu.roll(x, 1, x.ndim - 1)
    return jnp.where(lane % 2 == 0, nxt, prv)


def _retention_kernel(q_ref, k_ref, v_ref, g_ref, cos_ref, sin_ref, mask_ref, dq_ref,
                      dk_ref, gb_ref, o_ref, st_ref):
    i = pl.program_id(0)

    @pl.when(i == 0)
    def _():
        st_ref[...] = jnp.zeros_like(st_ref)

    cos = cos_ref[...]
    sin = sin_ref[...]
    hs = range(A_HEADS)

    def rot(ref, h):
        x = ref[:, h * A_DK:(h + 1) * A_DK].astype(F32)
        return x * cos + _swap_pairs(x) * sin

    q = [rot(q_ref, h) for h in hs]
    k = [rot(k_ref, h) * (A_DK ** -0.5) for h in hs]
    v = [v_ref[:, h * A_DV:(h + 1) * A_DV].astype(BF16) for h in hs]
    st = [st_ref[h] for h in hs]
    sc = [_dot_nt(q[h].astype(BF16), k[h].astype(BF16)) * mask_ref[h] for h in hs]
    oi = [_dot((q[h] * dq_ref[h]).astype(BF16), st[h].astype(BF16)) for h in hs]
    o = [_dot(sc[h].astype(BF16), v[h]) + oi[h] for h in hs]
    kv = [_dot_tn((k[h] * dk_ref[h]).astype(BF16), v[h]) for h in hs]
    for h in hs:
        st_ref[h] = gb_ref[h] * st[h] + kv[h]
        on = o[h] * lax.rsqrt(jnp.mean(o[h] * o[h], axis=-1, keepdims=True) + A_NORM_EPS)
        cols = slice(h * A_DV, (h + 1) * A_DV)
        o_ref[:, cols] = (on * _silu(g_ref[:, cols].astype(F32))).astype(o_ref.dtype)


def _retention_tables():
    h = np.arange(A_HEADS, dtype=np.float64)
    log_g = np.log(1.0 - 2.0 ** (-5.0 - h))
    pos = np.arange(A_TB)
    d = pos[:, None] - pos[None, :]
    cq = pos[:, None] // CHUNK
    ck = pos[None, :] // CHUNK
    same = cq == ck
    past = ck < cq
    expo = np.where(same, np.abs(d), np.where(past, d, 0)).astype(np.float64)
    mask = np.exp(log_g[:, None, None] * expo[None]) * (same | past)[None]
    dq = np.exp(log_g[:, None] * (pos + 1.0)[None, :])
    dk = np.exp(log_g[:, None] * (A_TB - 1.0 - pos)[None, :])
    gb = np.exp(log_g * A_TB)
    ones = np.ones((1, 1, LANES))
    return (jnp.asarray(mask, F32), jnp.asarray(dq[:, :, None] * ones, F32),
            jnp.asarray(dk[:, :, None] * ones, F32),
            jnp.asarray(gb[:, None, None] * np.ones((1, 1, A_DV)), F32))


def _rotary_tables(s):
    inv = 1.0 / (ROPE_BASE ** (jnp.arange(0, A_DK, 2, dtype=F32) / A_DK))
    ang = jnp.arange(s).astype(F32)[:, None] * inv[None, :]
    cos = jnp.repeat(jnp.cos(ang), 2, axis=-1)
    sin = jnp.repeat(jnp.sin(ang), 2, axis=-1)
    sign = jnp.asarray(np.tile(np.array([-1.0, 1.0]), A_DK // 2), F32)
    return cos, sin * sign[None, :]


def retention(ha, rot_tbl, ret_tbl, out_dtype):
    s = ha.shape[0]
    cos, sin = rot_tbl
    mask, dq, dk, gb = ret_tbl
    whole = lambda shape: pl.BlockSpec(shape, lambda i: (0,) * len(shape))
    return pl.pallas_call(
        _retention_kernel,
        grid=(s // A_TB,),
        in_specs=[pl.BlockSpec((A_TB, A_QK), lambda i: (i, 0)),
                  pl.BlockSpec((A_TB, A_QK), lambda i: (i, 1)),
                  pl.BlockSpec((A_TB, A_W), lambda i: (i, 2 * A_QK // A_W)),
                  pl.BlockSpec((A_TB, A_W), lambda i: (i, 2 * A_QK // A_W + 1)),
                  pl.BlockSpec((A_TB, A_DK), lambda i: (i, 0)),
                  pl.BlockSpec((A_TB, A_DK), lambda i: (i, 0)),
                  whole((A_HEADS, A_TB, A_TB)), whole((A_HEADS, A_TB, LANES)),
                  whole((A_HEADS, A_TB, LANES)), whole((A_HEADS, 1, A_DV))],
        out_specs=pl.BlockSpec((A_TB, A_W), lambda i: (i, 0)),
        out_shape=jax.ShapeDtypeStruct((s, A_W), out_dtype),
        scratch_shapes=[pltpu.VMEM((A_HEADS, A_DK, A_DV), F32)],
        compiler_params=_params(("arbitrary",)),
        name="retention",
    )(ha, ha, ha, ha, cos, sin, mask, dq, dk, gb)


B_TP = 512
B_TC = 512
B_NCB = B_TC // CHUNK
B_NPAIR = B_W // LANES


def _seg_ones():
    r = lax.broadcasted_iota(jnp.int32, (LANES, LANES), 0) // B_HEAD
    c = lax.broadcasted_iota(jnp.int32, (LANES, LANES), 1) // B_HEAD
    return (r == c).astype(BF16)


def _seg_sum(x):
    ones = _seg_ones()
    hi = x.astype(BF16)
    lo = (x - hi.astype(F32)).astype(BF16)
    parts = [_dot(hi[:, j:j + LANES], ones) + _dot(lo[:, j:j + LANES], ones)
             for j in range(0, x.shape[1], LANES)]
    return jnp.concatenate(parts, axis=1)


def _rwkv_prep_kernel(hb_ref, prev_ref, mu_ref, w0_ref, wup_ref, a0_ref, aup_ref, kk_ref,
                      ka_ref, rk_ref, rt_ref, kq_ref, bt_ref, kt_ref, v_ref, bte_ref,
                      kte_ref, bonus_ref, pl_ref):
    i = pl.program_id(0)
    hb = hb_ref[...]
    prev_row = jnp.where(i == 0, 0.0, prev_ref[7:8, :])
    rolled = pltpu.roll(hb, 1, 0)
    row8 = lax.broadcasted_iota(jnp.int32, (8, hb.shape[1]), 0)
    shifted = jnp.concatenate([jnp.where(row8 == 0, prev_row, rolled[0:8]), rolled[8:]], axis=0)
    xs = hb + (shifted - hb) * mu_ref[...]
    r = xs[:, 0:B_W]
    k = xs[:, B_W:2 * B_W]
    v = xs[:, 2 * B_W:3 * B_W]
    wd = xs[:, 3 * B_W:3 * B_W + B_RANK]
    ad = xs[:, 3 * B_W + B_RANK:3 * B_W + 2 * B_RANK]
    z = w0_ref[...] + _mm(jnp.tanh(wd), wup_ref[...])
    lw = -math.exp(-0.5) * jax.nn.sigmoid(z)
    a = jax.nn.sigmoid(a0_ref[...] + _mm(ad, aup_ref[...]))
    kk = k * kk_ref[...]
    kk = kk * lax.rsqrt(jnp.maximum(_seg_sum(kk * kk), 1e-24))
    kmod = k * (1.0 + (a - 1.0) * ka_ref[...])
    bonus_ref[...] = _seg_sum(r * kmod * rk_ref[...]) * v
    tr = lax.broadcasted_iota(jnp.int32, (B_TP, B_TP), 0)
    tc = lax.broadcasted_iota(jnp.int32, (B_TP, B_TP), 1)
    tri = ((tr // CHUNK == tc // CHUNK) & (tr >= tc)).astype(BF16)
    hi = lw.astype(BF16)
    rest = lw - hi.astype(F32)
    mid = rest.astype(BF16)
    lo = (rest - mid.astype(F32)).astype(BF16)
    cum = _dot(tri, hi) + _dot(tri, mid) + _dot(tri, lo)
    nc = B_TP // CHUNK
    ends = [cum[(c + 1) * CHUNK - 1:(c + 1) * CHUNK, :] for c in range(nc)]
    p_inv = jnp.exp(-cum)
    p_end = jnp.concatenate([jnp.exp(ends[c] - cum[c * CHUNK:(c + 1) * CHUNK, :])
                             for c in range(nc)], axis=0)
    rt_ref[...] = (r * jnp.exp(cum)).astype(BF16)
    kq_ref[...] = (kk * jnp.exp(cum - lw)).astype(BF16)
    bt_ref[...] = (kk * a * p_inv).astype(BF16)
    kt_ref[...] = (kmod * p_inv).astype(BF16)
    v_ref[...] = v.astype(BF16)
    bte_ref[...] = (kk * a * p_end).astype(BF16)
    kte_ref[...] = (kmod * p_end).astype(BF16)
    pl_ref[...] = jnp.exp(jnp.concatenate(ends, axis=0))


def rwkv_prep(hb, mu, w0, w_up, a0, a_up, k_k, k_a, r_k):
    s = hb.shape[0]
    nb = s // B_TP
    vec = lambda n: pl.BlockSpec((1, n), lambda i: (0, 0))
    full = lambda a, b: pl.BlockSpec((a, b), lambda i: (0, 0))
    big = pl.BlockSpec((B_TP, B_W), lambda i: (i, 0))
    outs = ([jax.ShapeDtypeStruct((s, B_W), BF16)] * 7
            + [jax.ShapeDtypeStruct((s, B_W), F32),
               jax.ShapeDtypeStruct((s // CHUNK, B_W), F32)])
    return pl.pallas_call(
        _rwkv_prep_kernel,
        grid=(nb,),
        in_specs=[pl.BlockSpec((B_TP, B_SHIFT_W), lambda i: (i, 0)),
                  pl.BlockSpec((8, B_SHIFT_W),
                               lambda i: (jnp.maximum(i * (B_TP // 8) - 1, 0), 0)),
                  vec(B_SHIFT_W), vec(B_W), full(B_RANK, B_W), vec(B_W), full(B_RANK, B_W),
                  vec(B_W), vec(B_W), vec(B_W)],
        out_specs=[big] * 8 + [pl.BlockSpec((B_TP // CHUNK, B_W), lambda i: (i, 0))],
        out_shape=outs,
        compiler_params=_params(("parallel",)),
        name="rwkv_prep",
    )(hb, hb, mu.reshape(1, -1), w0.reshape(1, -1), w_up, a0.reshape(1, -1), a_up,
      k_k.reshape(1, -1), k_a.reshape(1, -1), r_k.reshape(1, -1))


def _stack_heads(x):
    lane = lax.broadcasted_iota(jnp.int32, x.shape, 1)
    zero = jnp.zeros_like(x)
    return jnp.concatenate([jnp.where(lane < B_HEAD, x, zero),
                            jnp.where(lane >= B_HEAD, x, zero)], axis=0)


def _mm(a, b):
    return _dot(a.astype(BF16), b.astype(BF16))


def _rwkv_intra_kernel(rt_ref, kq_ref, bt_ref, kt_ref, v_ref, bte_ref, kte_ref, pl_ref,
                       mz_ref, gz_ref, rq_ref, yc_ref):
    n2 = 2 * CHUNK
    r_i = lax.broadcasted_iota(jnp.int32, (n2, n2), 0)
    c_i = lax.broadcasted_iota(jnp.int32, (n2, n2), 1)
    same_head = (r_i // CHUNK) == (c_i // CHUNK)
    strict = (same_head & (r_i > c_i)).astype(F32)
    incl = (same_head & (r_i >= c_i)).astype(F32)
    eye = (r_i == c_i).astype(F32)
    bd16 = ((r_i // 16) == (c_i // 16)).astype(F32)
    bd32 = ((r_i // 32) == (c_i // 32)).astype(F32)

    cs = range(B_NCB)

    def load(ref):
        return [_stack_heads(ref[c * CHUNK:(c + 1) * CHUNK, :]) for c in cs]

    def each(fn, *lists):
        return [fn(*args) for args in zip(*lists)]

    rt, kq, bt, kt, vv, bte, kte = (load(r) for r in (rt_ref, kq_ref, bt_ref, kt_ref, v_ref,
                                                      bte_ref, kte_ref))
    ab = each(lambda kq_, rt_, bt_, kt_: _dot_nt(jnp.concatenate([kq_, rt_], axis=0),
                                                 jnp.concatenate([bt_, kt_], axis=0)),
              kq, rt, bt, kt)
    a_m = [x[:n2, :n2] * strict for x in ab]
    b_m = [x[:n2, n2:] * strict for x in ab]
    aq = [x[n2:, :n2] * incl for x in ab]
    bq = [x[n2:, n2:] * incl for x in ab]
    nn = [-(x * bd16) for x in a_m]
    t = [eye + x for x in nn]
    sq = each(_mm, nn, nn)
    for step in range(3):
        t = each(lambda t_, s_: t_ + _mm(t_, s_), t, sq)
        if step < 2:
            sq = each(_mm, sq, sq)
    for blk in (bd32 - bd16, 1.0 - bd32):
        e = each(lambda a_, t_: _mm(a_ * blk, t_), a_m, t)
        t = each(lambda t_, e_: t_ - _mm(t_, e_), t, e)
    bv = each(_mm, b_m, vv)
    tw = each(lambda t_, kq_, bv_: _mm(t_, jnp.concatenate([kq_.astype(F32), bv_], axis=1)),
              t, kq, bv)
    w = [x[:, :n2].astype(BF16) for x in tw]
    u0 = [(-x[:, n2:]).astype(BF16) for x in tw]
    aw = each(lambda aq_, w_, u_: _mm(aq_, jnp.concatenate([w_, u_], axis=1)), aq, w, u0)
    bqv = each(_mm, bq, vv)
    wb = each(_dot_tn, w, bte)
    gz = each(lambda u_, v_, b_, k_: _dot_tn(jnp.concatenate([u_, v_], axis=0),
                                             jnp.concatenate([b_, k_], axis=0)),
              u0, vv, bte, kte)
    for c in cs:
        rq_ref[0, c] = (rt[c].astype(F32) - aw[c][:, :n2]).astype(BF16)
        yc2 = bqv[c] + aw[c][:, n2:]
        yc_ref[c * CHUNK:(c + 1) * CHUNK, :] = yc2[:CHUNK] + yc2[CHUNK:]
        mz_ref[0, c] = (eye * pl_ref[c:c + 1, :] - wb[c]).astype(BF16)
        gz_ref[0, c] = gz[c]


def rwkv_intra(prep):
    rt, kq, bt, kt, v, bte, kte, _, p_end = prep
    s = rt.shape[0]
    nchunk = s // CHUNK
    blk = pl.BlockSpec((B_TC, LANES), lambda p, i: (i, p))
    mat = pl.BlockSpec((1, B_NCB, LANES, LANES), lambda p, i: (p, i, 0, 0))
    mshape = (B_NPAIR, nchunk, LANES, LANES)
    return pl.pallas_call(
        _rwkv_intra_kernel,
        grid=(B_NPAIR, s // B_TC),
        in_specs=[blk] * 7 + [pl.BlockSpec((B_NCB, LANES), lambda p, i: (i, p))],
        out_specs=[mat, mat, mat, blk],
        out_shape=[jax.ShapeDtypeStruct(mshape, BF16), jax.ShapeDtypeStruct(mshape, F32),
                   jax.ShapeDtypeStruct(mshape, BF16), jax.ShapeDtypeStruct((s, B_W), F32)],
        compiler_params=_params(("parallel", "parallel")),
        name="rwkv_intra",
    )(rt, kq, bt, kt, v, bte, kte, p_end)


def _rwkv_serial_kernel(mz_ref, gz_ref, rq_ref, yc_ref, bonus_ref, g_ref, lng_ref, lnb_ref,
                        o_ref, zt_ref, y_ref):
    i = pl.program_id(0)

    @pl.when(i == 0)
    def _():
        zt_ref[...] = jnp.zeros_like(zt_ref)

    def chunk(c, carry):
        rows = pl.ds(pl.multiple_of(c * CHUNK, CHUNK), CHUNK)
        for p in range(B_NPAIR):
            cols = slice(p * LANES, (p + 1) * LANES)
            zt = zt_ref[p].astype(BF16)
            y2 = _dot_nt(rq_ref[p, c], zt)
            zt_ref[p] = _dot(zt, mz_ref[p, c]) + gz_ref[p, c]
            y_ref[rows, cols] = y2[:CHUNK] + y2[CHUNK:] + yc_ref[rows, cols]
        return carry

    lax.fori_loop(0, B_NCB, chunk, 0)
    y = y_ref[...]
    mean = _seg_sum(y) * (1.0 / B_HEAD)
    yc = y - mean
    var = _seg_sum(yc * yc) * (1.0 / B_HEAD)
    yn = yc * lax.rsqrt(var + B_LN_EPS) * lng_ref[...] + lnb_ref[...]
    o_ref[...] = ((yn + bonus_ref[...]) * _silu(g_ref[...].astype(F32))).astype(o_ref.dtype)


def rwkv_serial(intra, bonus, gate, ln_g, ln_b, out_dtype):
    mz, gz, rq, yc = intra
    s = yc.shape[0]
    mat = pl.BlockSpec((B_NPAIR, B_NCB, LANES, LANES), lambda i: (0, i, 0, 0))
    blk = pl.BlockSpec((B_TC, B_W), lambda i: (i, 0))
    vec = pl.BlockSpec((1, B_W), lambda i: (0, 0))
    return pl.pallas_call(
        _rwkv_serial_kernel,
        grid=(s // B_TC,),
        in_specs=[mat, mat, mat, blk, blk, blk, vec, vec],
        out_specs=blk,
        out_shape=jax.ShapeDtypeStruct((s, B_W), out_dtype),
        scratch_shapes=[pltpu.VMEM((B_NPAIR, LANES, LANES), F32),
                        pltpu.VMEM((B_TC, B_W), F32)],
        compiler_params=_params(("arbitrary",)),
        name="rwkv_serial",
    )(mz, gz, rq, yc, bonus, gate, ln_g.reshape(1, -1), ln_b.reshape(1, -1))


def rwkv7(hb, gate, mu, w0, w_up, a0, a_up, k_k, k_a, r_k, ln_g, ln_b, out_dtype):
    prep = rwkv_prep(hb, mu, w0, w_up, a0, a_up, k_k, k_a, r_k)
    intra = rwkv_intra(prep)
    return rwkv_serial(intra, prep[7], gate, ln_g, ln_b, out_dtype)


def kernel(x, norm_g, w_in, w_out, b_mu, b_w0, b_w_up, b_a0, b_a_up, b_k_k, b_k_a, b_r_k,
           b_ln_g, b_ln_b, c_rel_bias, final_g):
    bsz, s, d = x.shape
    assert bsz == 1 and d == D_MODEL
    xs = x.reshape(s, d)
    rot_tbl = _rotary_tables(s)
    ret_tbl = _retention_tables()
    bias_tbl = _attn_bias_tables(c_rel_bias)
    xg, ssq = prenorm(xs, norm_g[0])
    for l in range(DEPTH):
        ha, hb, gb, qkv, gc = inproj(xg, ssq, w_in, l)
        ya = retention(ha, rot_tbl, ret_tbl, BF16)
        yb = rwkv7(hb, gb, b_mu[l], b_w0[l], b_w_up[l], b_a0[l], b_a_up[l], b_k_k[l],
                   b_k_a[l], b_r_k[l].reshape(-1), b_ln_g[l], b_ln_b[l], BF16)
        yc = chunk_attention(qkv, gc, bias_tbl, l, BF16)
        if l + 1 < DEPTH:
            xs, xg, ssq = outproj(ya, yb, yc, w_out, l, xs, norm_g[l + 1])
        else:
            xs = outproj(ya, yb, yc, w_out, l, xs)
    return rmsnorm(xs, final_g, F32).reshape(bsz, s, d)
```

```python
import math

import numpy as np
import jax
import jax.numpy as jnp
from jax import lax
from jax.experimental import pallas as pl
from jax.experimental.pallas import tpu as pltpu

F32 = jnp.float32
BF16 = jnp.bfloat16

D_MODEL = 4096
DEPTH = 4
CHUNK = 64
A_W = 1024
A_HEADS = 4
A_DV = 256
A_DK = 128
A_QK = 512
A_IN = 2 * A_QK + 2 * A_W
ROPE_BASE = 10000.0
A_NORM_EPS = 1e-6
B_W = 1024
B_HEAD = 64
B_HEADS = 16
B_RANK = 128
B_SHIFT_W = 3 * B_W + 2 * B_RANK
B_LN_EPS = 64e-5
B_IN = B_SHIFT_W + B_W
C_W = 2048
C_HEADS = 16
C_DH = 128
C_BACK = 8
C_REL_CLIP = 128
RMS_EPS = 1e-6

LANES = 128
VMEM_LIMIT = 56 * 1024 * 1024

NEG = -1e30


def _params(sem):
    return pltpu.CompilerParams(dimension_semantics=sem, vmem_limit_bytes=VMEM_LIMIT)


def _dot(a, b, prec=None):
    return jnp.dot(a, b, preferred_element_type=F32, precision=prec)


def _dot_nt(a, b, prec=None):
    return lax.dot_general(a, b, (((1,), (1,)), ((), ())), preferred_element_type=F32,
                           precision=prec)


def _dot_tn(a, b, prec=None):
    return lax.dot_general(a, b, (((0,), (0,)), ((), ())), preferred_element_type=F32,
                           precision=prec)


def _silu(g):
    return g * jax.nn.sigmoid(g)


def _rmsnorm_kernel(x_ref, g_ref, o_ref):
    x = x_ref[...]
    ms = jnp.mean(x * x, axis=-1, keepdims=True)
    o_ref[...] = (x * lax.rsqrt(ms + RMS_EPS) * g_ref[...]).astype(o_ref.dtype)


def rmsnorm(x, g, out_dtype, tm=256):
    s, d = x.shape
    return pl.pallas_call(
        _rmsnorm_kernel,
        grid=(s // tm,),
        in_specs=[pl.BlockSpec((tm, d), lambda i: (i, 0)),
                  pl.BlockSpec((1, d), lambda i: (0, 0))],
        out_specs=pl.BlockSpec((tm, d), lambda i: (i, 0)),
        out_shape=jax.ShapeDtypeStruct((s, d), out_dtype),
        compiler_params=_params(("parallel",)),
        name="rmsnorm",
    )(x, g.reshape(1, d))


def _lane_partial_sq(x):
    sq = x * x
    acc = sq[:, 0:LANES]
    for j in range(LANES, x.shape[1], LANES):
        acc = acc + sq[:, j:j + LANES]
    return acc


def _prenorm_kernel(x_ref, g_ref, xg_ref, ssq_ref):
    x = x_ref[...]
    xg_ref[...] = (x * g_ref[...]).astype(xg_ref.dtype)
    ssq_ref[...] = _lane_partial_sq(x)


def prenorm(x, g, tm=256):
    s, d = x.shape
    return pl.pallas_call(
        _prenorm_kernel,
        grid=(s // tm,),
        in_specs=[pl.BlockSpec((tm, d), lambda i: (i, 0)),
                  pl.BlockSpec((1, d), lambda i: (0, 0))],
        out_specs=[pl.BlockSpec((tm, d), lambda i: (i, 0)),
                   pl.BlockSpec((tm, LANES), lambda i: (i, 0))],
        out_shape=[jax.ShapeDtypeStruct((s, d), BF16), jax.ShapeDtypeStruct((s, LANES), F32)],
        compiler_params=_params(("parallel",)),
        name="prenorm",
    )(x, g.reshape(1, d))


P_TM = 2048
P_TN = 256


def _resident_rows(shape):
    return pl.BlockSpec(shape, lambda i, j: (i, 0), pipeline_mode=pl.Buffered(1))


def _proj_segments():
    b0 = A_IN
    g0 = b0 + B_SHIFT_W
    c0 = g0 + B_W
    return ((0, A_IN, 0, None), (b0, B_SHIFT_W, 1, None), (g0, B_W, 2, None),
            (c0, C_W, 3, C_QSCALE), (c0 + C_W, 2 * C_W, 3, None), (c0 + 3 * C_W, C_W, 4, None))


def _inproj_kernel(a_ref, ssq_ref, w_ref, *refs):
    o_refs, r_ref = refs[:-1], refs[-1]
    j = pl.program_id(1)

    @pl.when(j == 0)
    def _():
        ms = jnp.sum(ssq_ref[...], axis=-1, keepdims=True) * (1.0 / D_MODEL)
        r_ref[...] = jnp.broadcast_to(lax.rsqrt(ms + RMS_EPS), r_ref.shape)

    for col0, width, slot, scale in _proj_segments():
        lo, hi = col0 // P_TN, (col0 + width) // P_TN

        @pl.when((j >= lo) & (j < hi))
        def _(slot=slot, scale=scale):
            r = r_ref[...]
            if scale is not None:
                r = r * scale
            acc = _dot(a_ref[...], w_ref[0].astype(BF16))
            acc = acc * jnp.concatenate([r] * (P_TN // LANES), axis=1)
            o_refs[slot][...] = acc.astype(o_refs[slot].dtype)


def inproj(xg, ssq, w_all, layer):
    m, k = xg.shape
    segs = _proj_segments()
    assert m % P_TM == 0 and all(c % P_TN == 0 and w % P_TN == 0 for c, w, _, _ in segs)
    dtypes = (F32, F32, F32, BF16, F32)
    out_specs, out_shape = [], []
    for slot, dt in enumerate(dtypes):
        lo = min(c for c, _, s, _ in segs if s == slot) // P_TN
        n = sum(w for _, w, s, _ in segs if s == slot) // P_TN
        out_specs.append(pl.BlockSpec(
            (P_TM, P_TN), lambda i, j, lo=lo, n=n: (i, jnp.clip(j - lo, 0, n - 1))))
        out_shape.append(jax.ShapeDtypeStruct((m, n * P_TN), dt))
    return pl.pallas_call(
        _inproj_kernel,
        grid=(m // P_TM, w_all.shape[2] // P_TN),
        in_specs=[_resident_rows((P_TM, k)), _resident_rows((P_TM, LANES)),
                  pl.BlockSpec((1, k, P_TN), lambda i, j: (layer, 0, j))],
        out_specs=out_specs,
        out_shape=out_shape,
        scratch_shapes=[pltpu.VMEM((P_TM, LANES), F32)],
        compiler_params=_params(("parallel", "arbitrary")),
        name="inproj",
    )(xg, ssq, w_all)


def _outproj_acc(ya_ref, yb_ref, yc_ref, wa_ref, wb_ref, wc_ref, x_ref):
    acc = _dot(ya_ref[...], wa_ref[0].astype(BF16))
    acc += _dot(yb_ref[...], wb_ref[0].astype(BF16))
    acc += _dot(yc_ref[...], wc_ref[0].astype(BF16))
    return x_ref[...] + acc


def _outproj_kernel(ya_ref, yb_ref, yc_ref, wa_ref, wb_ref, wc_ref, x_ref, o_ref):
    o_ref[...] = _outproj_acc(ya_ref, yb_ref, yc_ref, wa_ref, wb_ref, wc_ref, x_ref)


def _outproj_norm_kernel(ya_ref, yb_ref, yc_ref, wa_ref, wb_ref, wc_ref, x_ref, g_ref,
                         o_ref, xg_ref, ssq_ref):
    j = pl.program_id(1)
    xn = _outproj_acc(ya_ref, yb_ref, yc_ref, wa_ref, wb_ref, wc_ref, x_ref)
    o_ref[...] = xn
    xg_ref[...] = (xn * g_ref[...]).astype(xg_ref.dtype)
    part = _lane_partial_sq(xn)

    @pl.when(j == 0)
    def _():
        ssq_ref[...] = part

    @pl.when(j > 0)
    def _():
        ssq_ref[...] += part


def outproj(ya, yb, yc, w_all, layer, x, g_next=None):
    s = x.shape[0]
    n = w_all.shape[2]
    tm, tn = P_TM, P_TN
    tile = pl.BlockSpec((tm, tn), lambda i, j: (i, j))
    in_specs = [pl.BlockSpec((tm, A_W), lambda i, j: (i, 0)),
                pl.BlockSpec((tm, B_W), lambda i, j: (i, 0)),
                _resident_rows((tm, C_W)),
                pl.BlockSpec((1, A_W, tn), lambda i, j: (layer, 0, j)),
                pl.BlockSpec((1, B_W, tn), lambda i, j: (layer, A_W // B_W, j)),
                pl.BlockSpec((1, C_W, tn), lambda i, j: (layer, (A_W + B_W) // C_W, j)),
                tile]
    if g_next is None:
        return pl.pallas_call(
            _outproj_kernel,
            grid=(s // tm, n // tn),
            in_specs=in_specs,
            out_specs=tile,
            out_shape=jax.ShapeDtypeStruct((s, n), F32),
            compiler_params=_params(("parallel", "parallel")),
            name="outproj",
        )(ya, yb, yc, w_all, w_all, w_all, x)
    return pl.pallas_call(
        _outproj_norm_kernel,
        grid=(s // tm, n // tn),
        in_specs=in_specs + [pl.BlockSpec((1, tn), lambda i, j: (0, j))],
        out_specs=[tile, tile, pl.BlockSpec((tm, LANES), lambda i, j: (i, 0))],
        out_shape=[jax.ShapeDtypeStruct((s, n), F32), jax.ShapeDtypeStruct((s, n), BF16),
                   jax.ShapeDtypeStruct((s, LANES), F32)],
        compiler_params=_params(("parallel", "arbitrary")),
        name="outproj_norm",
    )(ya, yb, yc, w_all, w_all, w_all, x, g_next.reshape(1, n))


C_TQ = 256
C_NKB = C_BACK * CHUNK // C_TQ + 1
C_TK = C_NKB * C_TQ
C_TROW = C_TQ + C_TK


C_HPS = 8
LOG2E = math.log2(math.e)
C_QSCALE = C_DH ** -0.5 * LOG2E


def _attn_kernel(q_ref, k0_ref, k1_ref, k2_ref, v0_ref, v1_ref, v2_ref, g_ref, bias_ref,
                 o_ref):
    heads = [slice(h * C_DH, (h + 1) * C_DH) for h in range(C_HPS)]

    def scores(h):
        c = heads[h]
        kw = jnp.concatenate([k0_ref[:, c], k1_ref[:, c], k2_ref[:, c]], axis=0)
        return _dot_nt(q_ref[:, c], kw) + bias_ref[h]

    s = scores(0)
    for h, c in enumerate(heads):
        s_next = scores(h + 1) if h + 1 < C_HPS else None
        p = jnp.exp2(s - jnp.max(s, axis=-1, keepdims=True))
        l = jnp.sum(p, axis=-1, keepdims=True)
        vw = jnp.concatenate([v0_ref[:, c], v1_ref[:, c], v2_ref[:, c]], axis=0)
        o = _dot(p.astype(BF16), vw)
        o_ref[:, c] = (o / l * _silu(g_ref[:, c].astype(F32))).astype(o_ref.dtype)
        s = s_next


def _bias_kernel(r_ref, o_ref):
    x = jnp.broadcast_to(r_ref[0], (C_TQ, C_TROW))
    y = pltpu.roll(x, 0, 1, stride=1, stride_axis=0)[:, :C_TK]
    qc = lax.broadcasted_iota(jnp.int32, (C_TQ, C_TK), 0) // CHUNK
    kcol = lax.broadcasted_iota(jnp.int32, (C_TQ, C_TK), 1)
    dchunk = qc - (kcol // CHUNK - (C_TK - C_TQ) // CHUNK)
    band = (dchunk >= 0) & (dchunk <= C_BACK)
    for v in range(C_NKB):
        o_ref[v, 0] = jnp.where(band & (kcol >= (C_NKB - 1 - v) * C_TQ), y, NEG)


def _attn_bias_tables(rel_bias):
    j = np.arange(C_TROW)
    dist = (C_TK - j) % C_TROW - C_TQ
    idx = np.clip(dist, -(CHUNK - 1), C_REL_CLIP) + (CHUNK - 1)
    nl, nh = rel_bias.shape[0], rel_bias.shape[1]
    row = jnp.take(rel_bias, jnp.asarray(idx, jnp.int32), axis=-1) * LOG2E
    return pl.pallas_call(
        _bias_kernel,
        grid=(nl, nh),
        in_specs=[pl.BlockSpec((1, 1, C_TROW), lambda l, h: (l * nh + h, 0, 0))],
        out_specs=pl.BlockSpec((C_NKB, 1, C_TQ, C_TK), lambda l, h: (l, h, 0, 0)),
        out_shape=jax.ShapeDtypeStruct((nl * C_NKB, nh, C_TQ, C_TK), F32),
        compiler_params=_params(("parallel", "parallel")),
        name="attn_bias",
    )(row.reshape(nl * nh, 1, C_TROW))


def chunk_attention(qkv, gate, bias_tbl, layer, out_dtype):
    s = qkv.shape[0]
    nq = s // C_TQ
    wblk = C_HPS * C_DH
    ng = C_HEADS // C_HPS

    def kv_spec(base, back):
        return pl.BlockSpec((C_TQ, wblk),
                            lambda h, i: (jnp.maximum(i - back, 0), base + h))

    def bias_map(h, i):
        return (layer * C_NKB + jnp.minimum(i, C_NKB - 1), h, 0, 0)

    return pl.pallas_call(
        _attn_kernel,
        grid=(ng, nq),
        in_specs=[pl.BlockSpec((C_TQ, wblk), lambda h, i: (i, h)),
                  kv_spec(ng, 2), kv_spec(ng, 1), kv_spec(ng, 0),
                  kv_spec(2 * ng, 2), kv_spec(2 * ng, 1), kv_spec(2 * ng, 0),
                  pl.BlockSpec((C_TQ, wblk), lambda h, i: (i, h)),
                  pl.BlockSpec((None, C_HPS, C_TQ, C_TK), bias_map)],
        out_specs=pl.BlockSpec((C_TQ, wblk), lambda h, i: (i, h)),
        out_shape=jax.ShapeDtypeStruct((s, C_W), out_dtype),
        compiler_params=_params(("parallel", "parallel")),
        name="chunk_attn",
    )(qkv, qkv, qkv, qkv, qkv, qkv, qkv, gate, bias_tbl)


A_TB = 256


def _swap_pairs(x):
    lane = lax.broadcasted_iota(jnp.int32, x.shape, x.ndim - 1)
    nxt = pltpu.roll(x, LANES - 1, x.ndim - 1)
    prv = pltpu.roll(x, 1, x.ndim - 1)
    return jnp.where(lane % 2 == 0, nxt, prv)


def _retention_kernel(q_ref, k_ref, v_ref, g_ref, cos_ref, sin_ref, mask_ref, dq_ref,
                      dk_ref, gb_ref, o_ref, st_ref):
    i = pl.program_id(0)

    @pl.when(i == 0)
    def _():
        st_ref[...] = jnp.zeros_like(st_ref)

    cos = cos_ref[...]
    sin = sin_ref[...]
    hs = range(A_HEADS)

    def rot(ref, h):
        x = ref[:, h * A_DK:(h + 1) * A_DK].astype(F32)
        return x * cos + _swap_pairs(x) * sin

    q = [rot(q_ref, h) for h in hs]
    k = [rot(k_ref, h) * (A_DK ** -0.5) for h in hs]
    v = [v_ref[:, h * A_DV:(h + 1) * A_DV].astype(BF16) for h in hs]
    st = [st_ref[h] for h in hs]
    sc = [_dot_nt(q[h].astype(BF16), k[h].astype(BF16)) * mask_ref[h] for h in hs]
    oi = [_dot((q[h] * dq_ref[h]).astype(BF16), st[h].astype(BF16)) for h in hs]
    o = [_dot(sc[h].astype(BF16), v[h]) + oi[h] for h in hs]
    kv = [_dot_tn((k[h] * dk_ref[h]).astype(BF16), v[h]) for h in hs]
    for h in hs:
        st_ref[h] = gb_ref[h] * st[h] + kv[h]
        on = o[h] * lax.rsqrt(jnp.mean(o[h] * o[h], axis=-1, keepdims=True) + A_NORM_EPS)
        cols = slice(h * A_DV, (h + 1) * A_DV)
        o_ref[:, cols] = (on * _silu(g_ref[:, cols].astype(F32))).astype(o_ref.dtype)


def _retention_tables():
    h = np.arange(A_HEADS, dtype=np.float64)
    log_g = np.log(1.0 - 2.0 ** (-5.0 - h))
    pos = np.arange(A_TB)
    d = pos[:, None] - pos[None, :]
    cq = pos[:, None] // CHUNK
    ck = pos[None, :] // CHUNK
    same = cq == ck
    past = ck < cq
    expo = np.where(same, np.abs(d), np.where(past, d, 0)).astype(np.float64)
    mask = np.exp(log_g[:, None, None] * expo[None]) * (same | past)[None]
    dq = np.exp(log_g[:, None] * (pos + 1.0)[None, :])
    dk = np.exp(log_g[:, None] * (A_TB - 1.0 - pos)[None, :])
    gb = np.exp(log_g * A_TB)
    ones = np.ones((1, 1, LANES))
    return (jnp.asarray(mask, F32), jnp.asarray(dq[:, :, None] * ones, F32),
            jnp.asarray(dk[:, :, None] * ones, F32),
            jnp.asarray(gb[:, None, None] * np.ones((1, 1, A_DV)), F32))


def _rotary_tables(s):
    inv = 1.0 / (ROPE_BASE ** (jnp.arange(0, A_DK, 2, dtype=F32) / A_DK))
    ang = jnp.arange(s).astype(F32)[:, None] * inv[None, :]
    cos = jnp.repeat(jnp.cos(ang), 2, axis=-1)
    sin = jnp.repeat(jnp.sin(ang), 2, axis=-1)
    sign = jnp.asarray(np.tile(np.array([-1.0, 1.0]), A_DK // 2), F32)
    return cos, sin * sign[None, :]


def retention(ha, rot_tbl, ret_tbl, out_dtype):
    s = ha.shape[0]
    cos, sin = rot_tbl
    mask, dq, dk, gb = ret_tbl
    whole = lambda shape: pl.BlockSpec(shape, lambda i: (0,) * len(shape))
    return pl.pallas_call(
        _retention_kernel,
        grid=(s // A_TB,),
        in_specs=[pl.BlockSpec((A_TB, A_QK), lambda i: (i, 0)),
                  pl.BlockSpec((A_TB, A_QK), lambda i: (i, 1)),
                  pl.BlockSpec((A_TB, A_W), lambda i: (i, 2 * A_QK // A_W)),
                  pl.BlockSpec((A_TB, A_W), lambda i: (i, 2 * A_QK // A_W + 1)),
                  pl.BlockSpec((A_TB, A_DK), lambda i: (i, 0)),
                  pl.BlockSpec((A_TB, A_DK), lambda i: (i, 0)),
                  whole((A_HEADS, A_TB, A_TB)), whole((A_HEADS, A_TB, LANES)),
                  whole((A_HEADS, A_TB, LANES)), whole((A_HEADS, 1, A_DV))],
        out_specs=pl.BlockSpec((A_TB, A_W), lambda i: (i, 0)),
        out_shape=jax.ShapeDtypeStruct((s, A_W), out_dtype),
        scratch_shapes=[pltpu.VMEM((A_HEADS, A_DK, A_DV), F32)],
        compiler_params=_params(("arbitrary",)),
        name="retention",
    )(ha, ha, ha, ha, cos, sin, mask, dq, dk, gb)


B_TP = 512
B_TI = 2048
B_NCI = B_TI // CHUNK
B_TC = 512
B_NCB = B_TC // CHUNK
B_NPAIR = B_W // LANES


def _seg_ones():
    r = lax.broadcasted_iota(jnp.int32, (LANES, LANES), 0) // B_HEAD
    c = lax.broadcasted_iota(jnp.int32, (LANES, LANES), 1) // B_HEAD
    return (r == c).astype(BF16)


def _seg_sum(x):
    ones = _seg_ones()
    hi = x.astype(BF16)
    lo = (x - hi.astype(F32)).astype(BF16)
    parts = [_dot(hi[:, j:j + LANES], ones) + _dot(lo[:, j:j + LANES], ones)
             for j in range(0, x.shape[1], LANES)]
    return jnp.concatenate(parts, axis=1)


def _rwkv_prep_kernel(hb_ref, prev_ref, mu_ref, w0_ref, wup_ref, a0_ref, aup_ref, kk_ref,
                      ka_ref, rk_ref, rt_ref, kq_ref, bt_ref, kt_ref, v_ref, bte_ref,
                      kte_ref, bonus_ref, pl_ref):
    i = pl.program_id(0)
    hb = hb_ref[...]
    prev_row = jnp.where(i == 0, 0.0, prev_ref[7:8, :])
    rolled = pltpu.roll(hb, 1, 0)
    row8 = lax.broadcasted_iota(jnp.int32, (8, hb.shape[1]), 0)
    shifted = jnp.concatenate([jnp.where(row8 == 0, prev_row, rolled[0:8]), rolled[8:]], axis=0)
    xs = hb + (shifted - hb) * mu_ref[...]
    r = xs[:, 0:B_W]
    k = xs[:, B_W:2 * B_W]
    v = xs[:, 2 * B_W:3 * B_W]
    wd = xs[:, 3 * B_W:3 * B_W + B_RANK]
    ad = xs[:, 3 * B_W + B_RANK:3 * B_W + 2 * B_RANK]
    z = w0_ref[...] + _mm(jnp.tanh(wd), wup_ref[...])
    lw = -math.exp(-0.5) * jax.nn.sigmoid(z)
    a = jax.nn.sigmoid(a0_ref[...] + _mm(ad, aup_ref[...]))
    kk = k * kk_ref[...]
    kk = kk * lax.rsqrt(jnp.maximum(_seg_sum(kk * kk), 1e-24))
    kmod = k * (1.0 + (a - 1.0) * ka_ref[...])
    bonus_ref[...] = _seg_sum(r * kmod * rk_ref[...]) * v
    tr = lax.broadcasted_iota(jnp.int32, (B_TP, B_TP), 0)
    tc = lax.broadcasted_iota(jnp.int32, (B_TP, B_TP), 1)
    tri = ((tr // CHUNK == tc // CHUNK) & (tr >= tc)).astype(BF16)
    hi = lw.astype(BF16)
    rest = lw - hi.astype(F32)
    mid = rest.astype(BF16)
    lo = (rest - mid.astype(F32)).astype(BF16)
    cum = _dot(tri, hi) + _dot(tri, mid) + _dot(tri, lo)
    nc = B_TP // CHUNK
    ends = [cum[(c + 1) * CHUNK - 1:(c + 1) * CHUNK, :] for c in range(nc)]
    p_inv = jnp.exp(-cum)
    p_end = jnp.concatenate([jnp.exp(ends[c] - cum[c * CHUNK:(c + 1) * CHUNK, :])
                             for c in range(nc)], axis=0)
    rt_ref[...] = (r * jnp.exp(cum)).astype(BF16)
    kq_ref[...] = (kk * jnp.exp(cum - lw)).astype(BF16)
    bt_ref[...] = (kk * a * p_inv).astype(BF16)
    kt_ref[...] = (kmod * p_inv).astype(BF16)
    v_ref[...] = v.astype(BF16)
    bte_ref[...] = (kk * a * p_end).astype(BF16)
    kte_ref[...] = (kmod * p_end).astype(BF16)
    pl_ref[...] = jnp.exp(jnp.concatenate(ends, axis=0))


def rwkv_prep(hb, mu, w0, w_up, a0, a_up, k_k, k_a, r_k):
    s = hb.shape[0]
    nb = s // B_TP
    vec = lambda n: pl.BlockSpec((1, n), lambda i: (0, 0))
    full = lambda a, b: pl.BlockSpec((a, b), lambda i: (0, 0))
    big = pl.BlockSpec((B_TP, B_W), lambda i: (i, 0))
    outs = ([jax.ShapeDtypeStruct((s, B_W), BF16)] * 7
            + [jax.ShapeDtypeStruct((s, B_W), F32),
               jax.ShapeDtypeStruct((s // CHUNK, B_W), F32)])
    return pl.pallas_call(
        _rwkv_prep_kernel,
        grid=(nb,),
        in_specs=[pl.BlockSpec((B_TP, B_SHIFT_W), lambda i: (i, 0)),
                  pl.BlockSpec((8, B_SHIFT_W),
                               lambda i: (jnp.maximum(i * (B_TP // 8) - 1, 0), 0)),
                  vec(B_SHIFT_W), vec(B_W), full(B_RANK, B_W), vec(B_W), full(B_RANK, B_W),
                  vec(B_W), vec(B_W), vec(B_W)],
        out_specs=[big] * 8 + [pl.BlockSpec((B_TP // CHUNK, B_W), lambda i: (i, 0))],
        out_shape=outs,
        compiler_params=_params(("parallel",)),
        name="rwkv_prep",
    )(hb, hb, mu.reshape(1, -1), w0.reshape(1, -1), w_up, a0.reshape(1, -1), a_up,
      k_k.reshape(1, -1), k_a.reshape(1, -1), r_k.reshape(1, -1))


def _stack_heads(x):
    lane = lax.broadcasted_iota(jnp.int32, x.shape, 1)
    zero = jnp.zeros_like(x)
    return jnp.concatenate([jnp.where(lane < B_HEAD, x, zero),
                            jnp.where(lane >= B_HEAD, x, zero)], axis=0)


def _mm(a, b):
    return _dot(a.astype(BF16), b.astype(BF16))


def _rwkv_intra_kernel(rt_ref, kq_ref, bt_ref, kt_ref, v_ref, bte_ref, kte_ref, pl_ref,
                       mz_ref, gz_ref, rq_ref, yc_ref):
    n2 = 2 * CHUNK
    r_i = lax.broadcasted_iota(jnp.int32, (n2, n2), 0)
    c_i = lax.broadcasted_iota(jnp.int32, (n2, n2), 1)
    same_head = (r_i // CHUNK) == (c_i // CHUNK)
    strict = (same_head & (r_i > c_i)).astype(F32)
    incl = (same_head & (r_i >= c_i)).astype(F32)
    eye = (r_i == c_i).astype(F32)
    bd16 = ((r_i // 16) == (c_i // 16)).astype(F32)
    bd32 = ((r_i // 32) == (c_i // 32)).astype(F32)

    cs = range(B_NCI)

    def load(ref):
        return [_stack_heads(ref[c * CHUNK:(c + 1) * CHUNK, :]) for c in cs]

    def each(fn, *lists):
        return [fn(*args) for args in zip(*lists)]

    rt, kq, bt, kt, vv, bte, kte = (load(r) for r in (rt_ref, kq_ref, bt_ref, kt_ref, v_ref,
                                                      bte_ref, kte_ref))
    ab = each(lambda kq_, rt_, bt_, kt_: _dot_nt(jnp.concatenate([kq_, rt_], axis=0),
                                                 jnp.concatenate([bt_, kt_], axis=0)),
              kq, rt, bt, kt)
    a_m = [x[:n2, :n2] * strict for x in ab]
    b_m = [x[:n2, n2:] * strict for x in ab]
    aq = [x[n2:, :n2] * incl for x in ab]
    bq = [x[n2:, n2:] * incl for x in ab]
    nn = [-(x * bd16) for x in a_m]
    t = [eye + x for x in nn]
    sq = each(_mm, nn, nn)
    for step in range(3):
        t = each(lambda t_, s_: t_ + _mm(t_, s_), t, sq)
        if step < 2:
            sq = each(_mm, sq, sq)
    for blk in (bd32 - bd16, 1.0 - bd32):
        e = each(lambda a_, t_: _mm(a_ * blk, t_), a_m, t)
        t = each(lambda t_, e_: t_ - _mm(t_, e_), t, e)
    bv = each(_mm, b_m, vv)
    tw = each(lambda t_, kq_, bv_: _mm(t_, jnp.concatenate([kq_.astype(F32), bv_], axis=1)),
              t, kq, bv)
    w = [x[:, :n2].astype(BF16) for x in tw]
    u0 = [(-x[:, n2:]).astype(BF16) for x in tw]
    aw = each(lambda aq_, w_, u_: _mm(aq_, jnp.concatenate([w_, u_], axis=1)), aq, w, u0)
    bqv = each(_mm, bq, vv)
    wb = each(_dot_tn, w, bte)
    gz = each(lambda u_, v_, b_, k_: _dot_tn(jnp.concatenate([u_, v_], axis=0),
                                             jnp.concatenate([b_, k_], axis=0)),
              u0, vv, bte, kte)
    for c in cs:
        rq_ref[0, c] = (rt[c].astype(F32) - aw[c][:, :n2]).astype(BF16)
        yc2 = bqv[c] + aw[c][:, n2:]
        yc_ref[c * CHUNK:(c + 1) * CHUNK, :] = yc2[:CHUNK] + yc2[CHUNK:]
        mz_ref[0, c] = (eye * pl_ref[c:c + 1, :] - wb[c]).astype(BF16)
        gz_ref[0, c] = gz[c]


def rwkv_intra(prep):
    rt, kq, bt, kt, v, bte, kte, _, p_end = prep
    s = rt.shape[0]
    nchunk = s // CHUNK
    blk = pl.BlockSpec((B_TI, LANES), lambda p, i: (i, p))
    mat = pl.BlockSpec((1, B_NCI, LANES, LANES), lambda p, i: (p, i, 0, 0))
    mshape = (B_NPAIR, nchunk, LANES, LANES)
    return pl.pallas_call(
        _rwkv_intra_kernel,
        grid=(B_NPAIR, s // B_TI),
        in_specs=[blk] * 7 + [pl.BlockSpec((B_NCI, LANES), lambda p, i: (i, p))],
        out_specs=[mat, mat, mat, blk],
        out_shape=[jax.ShapeDtypeStruct(mshape, BF16), jax.ShapeDtypeStruct(mshape, F32),
                   jax.ShapeDtypeStruct(mshape, BF16), jax.ShapeDtypeStruct((s, B_W), F32)],
        compiler_params=_params(("parallel", "parallel")),
        name="rwkv_intra",
    )(rt, kq, bt, kt, v, bte, kte, p_end)


def _rwkv_serial_kernel(mz_ref, gz_ref, rq_ref, yc_ref, bonus_ref, g_ref, lng_ref, lnb_ref,
                        o_ref, zt_ref, y_ref):
    i = pl.program_id(0)

    @pl.when(i == 0)
    def _():
        zt_ref[...] = jnp.zeros_like(zt_ref)

    def chunk(c, carry):
        rows = pl.ds(pl.multiple_of(c * CHUNK, CHUNK), CHUNK)
        for p in range(B_NPAIR):
            cols = slice(p * LANES, (p + 1) * LANES)
            zt = zt_ref[p].astype(BF16)
            y2 = _dot_nt(rq_ref[p, c], zt)
            zt_ref[p] = _dot(zt, mz_ref[p, c]) + gz_ref[p, c]
            y_ref[rows, cols] = y2[:CHUNK] + y2[CHUNK:] + yc_ref[rows, cols]
        return carry

    lax.fori_loop(0, B_NCB, chunk, 0)
    y = y_ref[...]
    mean = _seg_sum(y) * (1.0 / B_HEAD)
    yc = y - mean
    var = _seg_sum(yc * yc) * (1.0 / B_HEAD)
    yn = yc * lax.rsqrt(var + B_LN_EPS) * lng_ref[...] + lnb_ref[...]
    o_ref[...] = ((yn + bonus_ref[...]) * _silu(g_ref[...].astype(F32))).astype(o_ref.dtype)


def rwkv_serial(intra, bonus, gate, ln_g, ln_b, out_dtype):
    mz, gz, rq, yc = intra
    s = yc.shape[0]
    mat = pl.BlockSpec((B_NPAIR, B_NCB, LANES, LANES), lambda i: (0, i, 0, 0))
    blk = pl.BlockSpec((B_TC, B_W), lambda i: (i, 0))
    vec = pl.BlockSpec((1, B_W), lambda i: (0, 0))
    return pl.pallas_call(
        _rwkv_serial_kernel,
        grid=(s // B_TC,),
        in_specs=[mat, mat, mat, blk, blk, blk, vec, vec],
        out_specs=blk,
        out_shape=jax.ShapeDtypeStruct((s, B_W), out_dtype),
        scratch_shapes=[pltpu.VMEM((B_NPAIR, LANES, LANES), F32),
                        pltpu.VMEM((B_TC, B_W), F32)],
        compiler_params=_params(("arbitrary",)),
        name="rwkv_serial",
    )(mz, gz, rq, yc, bonus, gate, ln_g.reshape(1, -1), ln_b.reshape(1, -1))


def rwkv7(hb, gate, mu, w0, w_up, a0, a_up, k_k, k_a, r_k, ln_g, ln_b, out_dtype):
    prep = rwkv_prep(hb, mu, w0, w_up, a0, a_up, k_k, k_a, r_k)
    intra = rwkv_intra(prep)
    return rwkv_serial(intra, prep[7], gate, ln_g, ln_b, out_dtype)


def kernel(x, norm_g, w_in, w_out, b_mu, b_w0, b_w_up, b_a0, b_a_up, b_k_k, b_k_a, b_r_k,
           b_ln_g, b_ln_b, c_rel_bias, final_g):
    bsz, s, d = x.shape
    assert bsz == 1 and d == D_MODEL
    xs = x.reshape(s, d)
    rot_tbl = _rotary_tables(s)
    ret_tbl = _retention_tables()
    bias_tbl = _attn_bias_tables(c_rel_bias)
    xg, ssq = prenorm(xs, norm_g[0])
    for l in range(DEPTH):
        ha, hb, gb, qkv, gc = inproj(xg, ssq, w_in, l)
        ya = retention(ha, rot_tbl, ret_tbl, BF16)
        yb = rwkv7(hb, gb, b_mu[l], b_w0[l], b_w_up[l], b_a0[l], b_a_up[l], b_k_k[l],
                   b_k_a[l], b_r_k[l].reshape(-1), b_ln_g[l], b_ln_b[l], BF16)
        yc = chunk_attention(qkv, gc, bias_tbl, l, BF16)
        if l + 1 < DEPTH:
            xs, xg, ssq = outproj(ya, yb, yc, w_out, l, xs, norm_g[l + 1])
        else:
            xs = outproj(ya, yb, yc, w_out, l, xs)
    return rmsnorm(xs, final_g, F32).reshape(bsz, s, d)
```

```python
import math

import numpy as np
import jax
import jax.numpy as jnp
from jax import lax
from jax.experimental import pallas as pl
from jax.experimental.pallas import tpu as pltpu

F32 = jnp.float32
BF16 = jnp.bfloat16

D_MODEL = 4096
DEPTH = 4
CHUNK = 64
A_W = 1024
A_HEADS = 4
A_DV = 256
A_DK = 128
A_QK = 512
A_IN = 2 * A_QK + 2 * A_W
ROPE_BASE = 10000.0
A_NORM_EPS = 1e-6
B_W = 1024
B_HEAD = 64
B_HEADS = 16
B_RANK = 128
B_SHIFT_W = 3 * B_W + 2 * B_RANK
B_LN_EPS = 64e-5
B_IN = B_SHIFT_W + B_W
C_W = 2048
C_HEADS = 16
C_DH = 128
C_BACK = 8
C_REL_CLIP = 128
RMS_EPS = 1e-6

LANES = 128
VMEM_LIMIT = 56 * 1024 * 1024

NEG = -1e30


def _params(sem):
    return pltpu.CompilerParams(dimension_semantics=sem, vmem_limit_bytes=VMEM_LIMIT)


def _dot(a, b, prec=None):
    return jnp.dot(a, b, preferred_element_type=F32, precision=prec)


def _dot_nt(a, b, prec=None):
    return lax.dot_general(a, b, (((1,), (1,)), ((), ())), preferred_element_type=F32,
                           precision=prec)


def _dot_tn(a, b, prec=None):
    return lax.dot_general(a, b, (((0,), (0,)), ((), ())), preferred_element_type=F32,
                           precision=prec)


def _silu(g):
    return g * jax.nn.sigmoid(g)


def _rmsnorm_kernel(x_ref, g_ref, o_ref):
    x = x_ref[...]
    ms = jnp.mean(x * x, axis=-1, keepdims=True)
    o_ref[...] = (x * lax.rsqrt(ms + RMS_EPS) * g_ref[...]).astype(o_ref.dtype)


def rmsnorm(x, g, out_dtype, tm=256):
    s, d = x.shape
    return pl.pallas_call(
        _rmsnorm_kernel,
        grid=(s // tm,),
        in_specs=[pl.BlockSpec((tm, d), lambda i: (i, 0)),
                  pl.BlockSpec((1, d), lambda i: (0, 0))],
        out_specs=pl.BlockSpec((tm, d), lambda i: (i, 0)),
        out_shape=jax.ShapeDtypeStruct((s, d), out_dtype),
        compiler_params=_params(("parallel",)),
        name="rmsnorm",
    )(x, g.reshape(1, d))


def _lane_partial_sq(x):
    sq = x * x
    acc = sq[:, 0:LANES]
    for j in range(LANES, x.shape[1], LANES):
        acc = acc + sq[:, j:j + LANES]
    return acc


def _prenorm_kernel(x_ref, g_ref, xg_ref, ssq_ref):
    x = x_ref[...]
    xg_ref[...] = (x * g_ref[...]).astype(xg_ref.dtype)
    ssq_ref[...] = _lane_partial_sq(x)


def prenorm(x, g, tm=256):
    s, d = x.shape
    return pl.pallas_call(
        _prenorm_kernel,
        grid=(s // tm,),
        in_specs=[pl.BlockSpec((tm, d), lambda i: (i, 0)),
                  pl.BlockSpec((1, d), lambda i: (0, 0))],
        out_specs=[pl.BlockSpec((tm, d), lambda i: (i, 0)),
                   pl.BlockSpec((tm, LANES), lambda i: (i, 0))],
        out_shape=[jax.ShapeDtypeStruct((s, d), BF16), jax.ShapeDtypeStruct((s, LANES), F32)],
        compiler_params=_params(("parallel",)),
        name="prenorm",
    )(x, g.reshape(1, d))


P_TM = 2048
P_TN = 256


def _resident_rows(shape):
    return pl.BlockSpec(shape, lambda i, j: (i, 0), pipeline_mode=pl.Buffered(1))


def _proj_segments():
    b0 = A_IN
    g0 = b0 + B_SHIFT_W
    c0 = g0 + B_W
    return ((0, A_IN, 0, None), (b0, B_SHIFT_W, 1, None), (g0, B_W, 2, None),
            (c0, C_W, 3, C_QSCALE), (c0 + C_W, 2 * C_W, 3, None), (c0 + 3 * C_W, C_W, 4, None))


def _inproj_kernel(a_ref, ssq_ref, w_ref, *refs):
    o_refs, r_ref = refs[:-1], refs[-1]
    j = pl.program_id(1)

    @pl.when(j == 0)
    def _():
        ms = jnp.sum(ssq_ref[...], axis=-1, keepdims=True) * (1.0 / D_MODEL)
        r_ref[...] = jnp.broadcast_to(lax.rsqrt(ms + RMS_EPS), r_ref.shape)

    for col0, width, slot, scale in _proj_segments():
        lo, hi = col0 // P_TN, (col0 + width) // P_TN

        @pl.when((j >= lo) & (j < hi))
        def _(slot=slot, scale=scale):
            r = r_ref[...]
            if scale is not None:
                r = r * scale
            acc = _dot(a_ref[...], w_ref[0].astype(BF16))
            acc = acc * jnp.concatenate([r] * (P_TN // LANES), axis=1)
            o_refs[slot][...] = acc.astype(o_refs[slot].dtype)


def inproj(xg, ssq, w_all, layer):
    m, k = xg.shape
    segs = _proj_segments()
    assert m % P_TM == 0 and all(c % P_TN == 0 and w % P_TN == 0 for c, w, _, _ in segs)
    dtypes = (F32, F32, F32, BF16, F32)
    out_specs, out_shape = [], []
    for slot, dt in enumerate(dtypes):
        lo = min(c for c, _, s, _ in segs if s == slot) // P_TN
        n = sum(w for _, w, s, _ in segs if s == slot) // P_TN
        out_specs.append(pl.BlockSpec(
            (P_TM, P_TN), lambda i, j, lo=lo, n=n: (i, jnp.clip(j - lo, 0, n - 1))))
        out_shape.append(jax.ShapeDtypeStruct((m, n * P_TN), dt))
    return pl.pallas_call(
        _inproj_kernel,
        grid=(m // P_TM, w_all.shape[2] // P_TN),
        in_specs=[_resident_rows((P_TM, k)), _resident_rows((P_TM, LANES)),
                  pl.BlockSpec((1, k, P_TN), lambda i, j: (layer, 0, j))],
        out_specs=out_specs,
        out_shape=out_shape,
        scratch_shapes=[pltpu.VMEM((P_TM, LANES), F32)],
        compiler_params=_params(("parallel", "arbitrary")),
        name="inproj",
    )(xg, ssq, w_all)


def _outproj_acc(ya_ref, yb_ref, yc_ref, wa_ref, wb_ref, wc_ref, x_ref):
    acc = _dot(ya_ref[...], wa_ref[0].astype(BF16))
    acc += _dot(yb_ref[...], wb_ref[0].astype(BF16))
    acc += _dot(yc_ref[...], wc_ref[0].astype(BF16))
    return x_ref[...] + acc


def _outproj_kernel(ya_ref, yb_ref, yc_ref, wa_ref, wb_ref, wc_ref, x_ref, o_ref):
    o_ref[...] = _outproj_acc(ya_ref, yb_ref, yc_ref, wa_ref, wb_ref, wc_ref, x_ref)


def _outproj_norm_kernel(ya_ref, yb_ref, yc_ref, wa_ref, wb_ref, wc_ref, x_ref, g_ref,
                         o_ref, xg_ref, ssq_ref):
    j = pl.program_id(1)
    xn = _outproj_acc(ya_ref, yb_ref, yc_ref, wa_ref, wb_ref, wc_ref, x_ref)
    o_ref[...] = xn
    xg_ref[...] = (xn * g_ref[...]).astype(xg_ref.dtype)
    part = _lane_partial_sq(xn)

    @pl.when(j == 0)
    def _():
        ssq_ref[...] = part

    @pl.when(j > 0)
    def _():
        ssq_ref[...] += part


def outproj(ya, yb, yc, w_all, layer, x, g_next=None):
    s = x.shape[0]
    n = w_all.shape[2]
    tm, tn = P_TM, P_TN
    tile = pl.BlockSpec((tm, tn), lambda i, j: (i, j))
    in_specs = [pl.BlockSpec((tm, A_W), lambda i, j: (i, 0)),
                pl.BlockSpec((tm, B_W), lambda i, j: (i, 0)),
                _resident_rows((tm, C_W)),
                pl.BlockSpec((1, A_W, tn), lambda i, j: (layer, 0, j)),
                pl.BlockSpec((1, B_W, tn), lambda i, j: (layer, A_W // B_W, j)),
                pl.BlockSpec((1, C_W, tn), lambda i, j: (layer, (A_W + B_W) // C_W, j)),
                tile]
    if g_next is None:
        return pl.pallas_call(
            _outproj_kernel,
            grid=(s // tm, n // tn),
            in_specs=in_specs,
            out_specs=tile,
            out_shape=jax.ShapeDtypeStruct((s, n), F32),
            compiler_params=_params(("parallel", "parallel")),
            name="outproj",
        )(ya, yb, yc, w_all, w_all, w_all, x)
    return pl.pallas_call(
        _outproj_norm_kernel,
        grid=(s // tm, n // tn),
        in_specs=in_specs + [pl.BlockSpec((1, tn), lambda i, j: (0, j))],
        out_specs=[tile, tile, pl.BlockSpec((tm, LANES), lambda i, j: (i, 0))],
        out_shape=[jax.ShapeDtypeStruct((s, n), F32), jax.ShapeDtypeStruct((s, n), BF16),
                   jax.ShapeDtypeStruct((s, LANES), F32)],
        compiler_params=_params(("parallel", "arbitrary")),
        name="outproj_norm",
    )(ya, yb, yc, w_all, w_all, w_all, x, g_next.reshape(1, n))


C_TQ = 256
C_NKB = C_BACK * CHUNK // C_TQ + 1
C_TK = C_NKB * C_TQ
C_TROW = C_TQ + C_TK


C_HPS = 16
LOG2E = math.log2(math.e)
C_QSCALE = C_DH ** -0.5 * LOG2E


def _attn_kernel(q_ref, k0_ref, k1_ref, k2_ref, v0_ref, v1_ref, v2_ref, g_ref, bias_ref,
                 o_ref):
    heads = [slice(h * C_DH, (h + 1) * C_DH) for h in range(C_HPS)]

    def scores(h):
        c = heads[h]
        kw = jnp.concatenate([k0_ref[:, c], k1_ref[:, c], k2_ref[:, c]], axis=0)
        return _dot_nt(q_ref[:, c], kw) + bias_ref[h]

    s = scores(0)
    for h, c in enumerate(heads):
        s_next = scores(h + 1) if h + 1 < C_HPS else None
        p = jnp.exp2(s - jnp.max(s, axis=-1, keepdims=True))
        l = jnp.sum(p, axis=-1, keepdims=True)
        vw = jnp.concatenate([v0_ref[:, c], v1_ref[:, c], v2_ref[:, c]], axis=0)
        o = _dot(p.astype(BF16), vw)
        o_ref[:, c] = (o / l * _silu(g_ref[:, c].astype(F32))).astype(o_ref.dtype)
        s = s_next


def _bias_kernel(r_ref, o_ref):
    x = jnp.broadcast_to(r_ref[0], (C_TQ, C_TROW))
    y = pltpu.roll(x, 0, 1, stride=1, stride_axis=0)[:, :C_TK]
    qc = lax.broadcasted_iota(jnp.int32, (C_TQ, C_TK), 0) // CHUNK
    kcol = lax.broadcasted_iota(jnp.int32, (C_TQ, C_TK), 1)
    dchunk = qc - (kcol // CHUNK - (C_TK - C_TQ) // CHUNK)
    band = (dchunk >= 0) & (dchunk <= C_BACK)
    for v in range(C_NKB):
        o_ref[v, 0] = jnp.where(band & (kcol >= (C_NKB - 1 - v) * C_TQ), y, NEG)


def _attn_bias_tables(rel_bias):
    j = np.arange(C_TROW)
    dist = (C_TK - j) % C_TROW - C_TQ
    idx = np.clip(dist, -(CHUNK - 1), C_REL_CLIP) + (CHUNK - 1)
    nl, nh = rel_bias.shape[0], rel_bias.shape[1]
    row = jnp.take(rel_bias, jnp.asarray(idx, jnp.int32), axis=-1) * LOG2E
    return pl.pallas_call(
        _bias_kernel,
        grid=(nl, nh),
        in_specs=[pl.BlockSpec((1, 1, C_TROW), lambda l, h: (l * nh + h, 0, 0))],
        out_specs=pl.BlockSpec((C_NKB, 1, C_TQ, C_TK), lambda l, h: (l, h, 0, 0)),
        out_shape=jax.ShapeDtypeStruct((nl * C_NKB, nh, C_TQ, C_TK), F32),
        compiler_params=_params(("parallel", "parallel")),
        name="attn_bias",
    )(row.reshape(nl * nh, 1, C_TROW))


def chunk_attention(qkv, gate, bias_tbl, layer, out_dtype):
    s = qkv.shape[0]
    nq = s // C_TQ
    wblk = C_HPS * C_DH
    ng = C_HEADS // C_HPS

    def kv_spec(base, back):
        return pl.BlockSpec((C_TQ, wblk),
                            lambda h, i: (jnp.maximum(i - back, 0), base + h))

    def bias_map(h, i):
        return (layer * C_NKB + jnp.minimum(i, C_NKB - 1), h, 0, 0)

    return pl.pallas_call(
        _attn_kernel,
        grid=(ng, nq),
        in_specs=[pl.BlockSpec((C_TQ, wblk), lambda h, i: (i, h)),
                  kv_spec(ng, 2), kv_spec(ng, 1), kv_spec(ng, 0),
                  kv_spec(2 * ng, 2), kv_spec(2 * ng, 1), kv_spec(2 * ng, 0),
                  pl.BlockSpec((C_TQ, wblk), lambda h, i: (i, h)),
                  pl.BlockSpec((None, C_HPS, C_TQ, C_TK), bias_map)],
        out_specs=pl.BlockSpec((C_TQ, wblk), lambda h, i: (i, h)),
        out_shape=jax.ShapeDtypeStruct((s, C_W), out_dtype),
        compiler_params=_params(("parallel", "parallel")),
        name="chunk_attn",
    )(qkv, qkv, qkv, qkv, qkv, qkv, qkv, gate, bias_tbl)


A_TB = 256


def _swap_pairs(x):
    lane = lax.broadcasted_iota(jnp.int32, x.shape, x.ndim - 1)
    nxt = pltpu.roll(x, LANES - 1, x.ndim - 1)
    prv = pltpu.roll(x, 1, x.ndim - 1)
    return jnp.where(lane % 2 == 0, nxt, prv)


def _retention_kernel(q_ref, k_ref, v_ref, g_ref, cos_ref, sin_ref, mask_ref, dq_ref,
                      dk_ref, gb_ref, o_ref, st_ref):
    i = pl.program_id(0)

    @pl.when(i == 0)
    def _():
        st_ref[...] = jnp.zeros_like(st_ref)

    cos = cos_ref[...]
    sin = sin_ref[...]
    hs = range(A_HEADS)

    def rot(ref, h):
        x = ref[:, h * A_DK:(h + 1) * A_DK].astype(F32)
        return x * cos + _swap_pairs(x) * sin

    q = [rot(q_ref, h) for h in hs]
    k = [rot(k_ref, h) * (A_DK ** -0.5) for h in hs]
    v = [v_ref[:, h * A_DV:(h + 1) * A_DV].astype(BF16) for h in hs]
    st = [st_ref[h] for h in hs]
    sc = [_dot_nt(q[h].astype(BF16), k[h].astype(BF16)) * mask_ref[h] for h in hs]
    oi = [_dot((q[h] * dq_ref[h]).astype(BF16), st[h].astype(BF16)) for h in hs]
    o = [_dot(sc[h].astype(BF16), v[h]) + oi[h] for h in hs]
    kv = [_dot_tn((k[h] * dk_ref[h]).astype(BF16), v[h]) for h in hs]
    for h in hs:
        st_ref[h] = gb_ref[h] * st[h] + kv[h]
        on = o[h] * lax.rsqrt(jnp.mean(o[h] * o[h], axis=-1, keepdims=True) + A_NORM_EPS)
        cols = slice(h * A_DV, (h + 1) * A_DV)
        o_ref[:, cols] = (on * _silu(g_ref[:, cols].astype(F32))).astype(o_ref.dtype)


def _retention_tables():
    h = np.arange(A_HEADS, dtype=np.float64)
    log_g = np.log(1.0 - 2.0 ** (-5.0 - h))
    pos = np.arange(A_TB)
    d = pos[:, None] - pos[None, :]
    cq = pos[:, None] // CHUNK
    ck = pos[None, :] // CHUNK
    same = cq == ck
    past = ck < cq
    expo = np.where(same, np.abs(d), np.where(past, d, 0)).astype(np.float64)
    mask = np.exp(log_g[:, None, None] * expo[None]) * (same | past)[None]
    dq = np.exp(log_g[:, None] * (pos + 1.0)[None, :])
    dk = np.exp(log_g[:, None] * (A_TB - 1.0 - pos)[None, :])
    gb = np.exp(log_g * A_TB)
    ones = np.ones((1, 1, LANES))
    return (jnp.asarray(mask, F32), jnp.asarray(dq[:, :, None] * ones, F32),
            jnp.asarray(dk[:, :, None] * ones, F32),
            jnp.asarray(gb[:, None, None] * np.ones((1, 1, A_DV)), F32))


def _rotary_tables(s):
    inv = 1.0 / (ROPE_BASE ** (jnp.arange(0, A_DK, 2, dtype=F32) / A_DK))
    ang = jnp.arange(s).astype(F32)[:, None] * inv[None, :]
    cos = jnp.repeat(jnp.cos(ang), 2, axis=-1)
    sin = jnp.repeat(jnp.sin(ang), 2, axis=-1)
    sign = jnp.asarray(np.tile(np.array([-1.0, 1.0]), A_DK // 2), F32)
    return cos, sin * sign[None, :]


def retention(ha, rot_tbl, ret_tbl, out_dtype):
    s = ha.shape[0]
    cos, sin = rot_tbl
    mask, dq, dk, gb = ret_tbl
    whole = lambda shape: pl.BlockSpec(shape, lambda i: (0,) * len(shape))
    return pl.pallas_call(
        _retention_kernel,
        grid=(s // A_TB,),
        in_specs=[pl.BlockSpec((A_TB, A_QK), lambda i: (i, 0)),
                  pl.BlockSpec((A_TB, A_QK), lambda i: (i, 1)),
                  pl.BlockSpec((A_TB, A_W), lambda i: (i, 2 * A_QK // A_W)),
                  pl.BlockSpec((A_TB, A_W), lambda i: (i, 2 * A_QK // A_W + 1)),
                  pl.BlockSpec((A_TB, A_DK), lambda i: (i, 0)),
                  pl.BlockSpec((A_TB, A_DK), lambda i: (i, 0)),
                  whole((A_HEADS, A_TB, A_TB)), whole((A_HEADS, A_TB, LANES)),
                  whole((A_HEADS, A_TB, LANES)), whole((A_HEADS, 1, A_DV))],
        out_specs=pl.BlockSpec((A_TB, A_W), lambda i: (i, 0)),
        out_shape=jax.ShapeDtypeStruct((s, A_W), out_dtype),
        scratch_shapes=[pltpu.VMEM((A_HEADS, A_DK, A_DV), F32)],
        compiler_params=_params(("arbitrary",)),
        name="retention",
    )(ha, ha, ha, ha, cos, sin, mask, dq, dk, gb)


B_TP = 512
B_TI = 2048
B_NCI = B_TI // CHUNK
B_TC = 512
B_NCB = B_TC // CHUNK
B_NPAIR = B_W // LANES


def _seg_ones():
    r = lax.broadcasted_iota(jnp.int32, (LANES, LANES), 0) // B_HEAD
    c = lax.broadcasted_iota(jnp.int32, (LANES, LANES), 1) // B_HEAD
    return (r == c).astype(BF16)


def _seg_sum(x):
    ones = _seg_ones()
    hi = x.astype(BF16)
    lo = (x - hi.astype(F32)).astype(BF16)
    parts = [_dot(hi[:, j:j + LANES], ones) + _dot(lo[:, j:j + LANES], ones)
             for j in range(0, x.shape[1], LANES)]
    return jnp.concatenate(parts, axis=1)


def _rwkv_prep_kernel(hb_ref, prev_ref, mu_ref, w0_ref, wup_ref, a0_ref, aup_ref, kk_ref,
                      ka_ref, rk_ref, rt_ref, kq_ref, bt_ref, kt_ref, v_ref, bte_ref,
                      kte_ref, bonus_ref, pl_ref):
    i = pl.program_id(0)
    hb = hb_ref[...]
    prev_row = jnp.where(i == 0, 0.0, prev_ref[7:8, :])
    rolled = pltpu.roll(hb, 1, 0)
    row8 = lax.broadcasted_iota(jnp.int32, (8, hb.shape[1]), 0)
    shifted = jnp.concatenate([jnp.where(row8 == 0, prev_row, rolled[0:8]), rolled[8:]], axis=0)
    xs = hb + (shifted - hb) * mu_ref[...]
    r = xs[:, 0:B_W]
    k = xs[:, B_W:2 * B_W]
    v = xs[:, 2 * B_W:3 * B_W]
    wd = xs[:, 3 * B_W:3 * B_W + B_RANK]
    ad = xs[:, 3 * B_W + B_RANK:3 * B_W + 2 * B_RANK]
    z = w0_ref[...] + _mm(jnp.tanh(wd), wup_ref[...])
    lw = -math.exp(-0.5) * jax.nn.sigmoid(z)
    a = jax.nn.sigmoid(a0_ref[...] + _mm(ad, aup_ref[...]))
    kk = k * kk_ref[...]
    kk = kk * lax.rsqrt(jnp.maximum(_seg_sum(kk * kk), 1e-24))
    kmod = k * (1.0 + (a - 1.0) * ka_ref[...])
    bonus_ref[...] = _seg_sum(r * kmod * rk_ref[...]) * v
    tr = lax.broadcasted_iota(jnp.int32, (B_TP, B_TP), 0)
    tc = lax.broadcasted_iota(jnp.int32, (B_TP, B_TP), 1)
    tri = ((tr // CHUNK == tc // CHUNK) & (tr >= tc)).astype(BF16)
    hi = lw.astype(BF16)
    rest = lw - hi.astype(F32)
    mid = rest.astype(BF16)
    lo = (rest - mid.astype(F32)).astype(BF16)
    cum = _dot(tri, hi) + _dot(tri, mid) + _dot(tri, lo)
    nc = B_TP // CHUNK
    ends = [cum[(c + 1) * CHUNK - 1:(c + 1) * CHUNK, :] for c in range(nc)]
    p_inv = jnp.exp(-cum)
    p_end = jnp.concatenate([jnp.exp(ends[c] - cum[c * CHUNK:(c + 1) * CHUNK, :])
                             for c in range(nc)], axis=0)
    rt_ref[...] = (r * jnp.exp(cum)).astype(BF16)
    kq_ref[...] = (kk * jnp.exp(cum - lw)).astype(BF16)
    bt_ref[...] = (kk * a * p_inv).astype(BF16)
    kt_ref[...] = (kmod * p_inv).astype(BF16)
    v_ref[...] = v.astype(BF16)
    bte_ref[...] = (kk * a * p_end).astype(BF16)
    kte_ref[...] = (kmod * p_end).astype(BF16)
    pl_ref[...] = jnp.exp(jnp.concatenate(ends, axis=0))


def rwkv_prep(hb, mu, w0, w_up, a0, a_up, k_k, k_a, r_k):
    s = hb.shape[0]
    nb = s // B_TP
    vec = lambda n: pl.BlockSpec((1, n), lambda i: (0, 0))
    full = lambda a, b: pl.BlockSpec((a, b), lambda i: (0, 0))
    big = pl.BlockSpec((B_TP, B_W), lambda i: (i, 0))
    outs = ([jax.ShapeDtypeStruct((s, B_W), BF16)] * 7
            + [jax.ShapeDtypeStruct((s, B_W), F32),
               jax.ShapeDtypeStruct((s // CHUNK, B_W), F32)])
    return pl.pallas_call(
        _rwkv_prep_kernel,
        grid=(nb,),
        in_specs=[pl.BlockSpec((B_TP, B_SHIFT_W), lambda i: (i, 0)),
                  pl.BlockSpec((8, B_SHIFT_W),
                               lambda i: (jnp.maximum(i * (B_TP // 8) - 1, 0), 0)),
                  vec(B_SHIFT_W), vec(B_W), full(B_RANK, B_W), vec(B_W), full(B_RANK, B_W),
                  vec(B_W), vec(B_W), vec(B_W)],
        out_specs=[big] * 8 + [pl.BlockSpec((B_TP // CHUNK, B_W), lambda i: (i, 0))],
        out_shape=outs,
        compiler_params=_params(("parallel",)),
        name="rwkv_prep",
    )(hb, hb, mu.reshape(1, -1), w0.reshape(1, -1), w_up, a0.reshape(1, -1), a_up,
      k_k.reshape(1, -1), k_a.reshape(1, -1), r_k.reshape(1, -1))


def _stack_heads(x):
    lane = lax.broadcasted_iota(jnp.int32, x.shape, 1)
    zero = jnp.zeros_like(x)
    return jnp.concatenate([jnp.where(lane < B_HEAD, x, zero),
                            jnp.where(lane >= B_HEAD, x, zero)], axis=0)


def _mm(a, b):
    return _dot(a.astype(BF16), b.astype(BF16))


def _rwkv_intra_kernel(rt_ref, kq_ref, bt_ref, kt_ref, v_ref, bte_ref, kte_ref, pl_ref,
                       mz_ref, gz_ref, rq_ref, yc_ref):
    n2 = 2 * CHUNK
    r_i = lax.broadcasted_iota(jnp.int32, (n2, n2), 0)
    c_i = lax.broadcasted_iota(jnp.int32, (n2, n2), 1)
    same_head = (r_i // CHUNK) == (c_i // CHUNK)
    strict = (same_head & (r_i > c_i)).astype(F32)
    incl = (same_head & (r_i >= c_i)).astype(F32)
    eye = (r_i == c_i).astype(F32)
    bd16 = ((r_i // 16) == (c_i // 16)).astype(F32)
    bd32 = ((r_i // 32) == (c_i // 32)).astype(F32)

    cs = range(B_NCI)

    def load(ref):
        return [_stack_heads(ref[c * CHUNK:(c + 1) * CHUNK, :]) for c in cs]

    def each(fn, *lists):
        return [fn(*args) for args in zip(*lists)]

    rt, kq, bt, kt, vv, bte, kte = (load(r) for r in (rt_ref, kq_ref, bt_ref, kt_ref, v_ref,
                                                      bte_ref, kte_ref))
    ab = each(lambda kq_, rt_, bt_, kt_: _dot_nt(jnp.concatenate([kq_, rt_], axis=0),
                                                 jnp.concatenate([bt_, kt_], axis=0)),
              kq, rt, bt, kt)
    a_m = [x[:n2, :n2] * strict for x in ab]
    b_m = [x[:n2, n2:] * strict for x in ab]
    aq = [x[n2:, :n2] * incl for x in ab]
    bq = [x[n2:, n2:] * incl for x in ab]
    nn = [-(x * bd16) for x in a_m]
    t = [eye + x for x in nn]
    sq = each(_mm, nn, nn)
    for step in range(3):
        t = each(lambda t_, s_: t_ + _mm(t_, s_), t, sq)
        if step < 2:
            sq = each(_mm, sq, sq)
    for blk in (bd32 - bd16, 1.0 - bd32):
        e = each(lambda a_, t_: _mm(a_ * blk, t_), a_m, t)
        t = each(lambda t_, e_: t_ - _mm(t_, e_), t, e)
    bv = each(_mm, b_m, vv)
    tw = each(lambda t_, kq_, bv_: _mm(t_, jnp.concatenate([kq_.astype(F32), bv_], axis=1)),
              t, kq, bv)
    w = [x[:, :n2].astype(BF16) for x in tw]
    u0 = [(-x[:, n2:]).astype(BF16) for x in tw]
    aw = each(lambda aq_, w_, u_: _mm(aq_, jnp.concatenate([w_, u_], axis=1)), aq, w, u0)
    bqv = each(_mm, bq, vv)
    wb = each(_dot_tn, w, bte)
    gz = each(lambda u_, v_, b_, k_: _dot_tn(jnp.concatenate([u_, v_], axis=0),
                                             jnp.concatenate([b_, k_], axis=0)),
              u0, vv, bte, kte)
    for c in cs:
        rq_ref[0, c] = (rt[c].astype(F32) - aw[c][:, :n2]).astype(BF16)
        yc2 = bqv[c] + aw[c][:, n2:]
        yc_ref[c * CHUNK:(c + 1) * CHUNK, :] = yc2[:CHUNK] + yc2[CHUNK:]
        mz_ref[0, c] = (eye * pl_ref[c:c + 1, :] - wb[c]).astype(BF16)
        gz_ref[0, c] = gz[c]


def rwkv_intra(prep):
    rt, kq, bt, kt, v, bte, kte, _, p_end = prep
    s = rt.shape[0]
    nchunk = s // CHUNK
    blk = pl.BlockSpec((B_TI, LANES), lambda p, i: (i, p))
    mat = pl.BlockSpec((1, B_NCI, LANES, LANES), lambda p, i: (p, i, 0, 0))
    mshape = (B_NPAIR, nchunk, LANES, LANES)
    return pl.pallas_call(
        _rwkv_intra_kernel,
        grid=(B_NPAIR, s // B_TI),
        in_specs=[blk] * 7 + [pl.BlockSpec((B_NCI, LANES), lambda p, i: (i, p))],
        out_specs=[mat, mat, mat, blk],
        out_shape=[jax.ShapeDtypeStruct(mshape, BF16), jax.ShapeDtypeStruct(mshape, F32),
                   jax.ShapeDtypeStruct(mshape, BF16), jax.ShapeDtypeStruct((s, B_W), F32)],
        compiler_params=_params(("parallel", "parallel")),
        name="rwkv_intra",
    )(rt, kq, bt, kt, v, bte, kte, p_end)


def _rwkv_serial_kernel(mz_ref, gz_ref, rq_ref, yc_ref, bonus_ref, g_ref, lng_ref, lnb_ref,
                        o_ref, zt_ref, y_ref):
    i = pl.program_id(0)

    @pl.when(i == 0)
    def _():
        zt_ref[...] = jnp.zeros_like(zt_ref)

    zt = [zt_ref[p] for p in range(B_NPAIR)]
    for c in range(B_NCB):
        rows = slice(c * CHUNK, (c + 1) * CHUNK)
        zb = [z.astype(BF16) for z in zt]
        y2 = [_dot_nt(rq_ref[p, c], zb[p]) for p in range(B_NPAIR)]
        zt = [_dot(zb[p], mz_ref[p, c]) + gz_ref[p, c] for p in range(B_NPAIR)]
        for p in range(B_NPAIR):
            cols = slice(p * LANES, (p + 1) * LANES)
            y_ref[rows, cols] = y2[p][:CHUNK] + y2[p][CHUNK:] + yc_ref[rows, cols]
    for p in range(B_NPAIR):
        zt_ref[p] = zt[p]
    y = y_ref[...]
    mean = _seg_sum(y) * (1.0 / B_HEAD)
    yc = y - mean
    var = _seg_sum(yc * yc) * (1.0 / B_HEAD)
    yn = yc * lax.rsqrt(var + B_LN_EPS) * lng_ref[...] + lnb_ref[...]
    o_ref[...] = ((yn + bonus_ref[...]) * _silu(g_ref[...].astype(F32))).astype(o_ref.dtype)


def rwkv_serial(intra, bonus, gate, ln_g, ln_b, out_dtype):
    mz, gz, rq, yc = intra
    s = yc.shape[0]
    mat = pl.BlockSpec((B_NPAIR, B_NCB, LANES, LANES), lambda i: (0, i, 0, 0))
    blk = pl.BlockSpec((B_TC, B_W), lambda i: (i, 0))
    vec = pl.BlockSpec((1, B_W), lambda i: (0, 0))
    return pl.pallas_call(
        _rwkv_serial_kernel,
        grid=(s // B_TC,),
        in_specs=[mat, mat, mat, blk, blk, blk, vec, vec],
        out_specs=blk,
        out_shape=jax.ShapeDtypeStruct((s, B_W), out_dtype),
        scratch_shapes=[pltpu.VMEM((B_NPAIR, LANES, LANES), F32),
                        pltpu.VMEM((B_TC, B_W), F32)],
        compiler_params=_params(("arbitrary",)),
        name="rwkv_serial",
    )(mz, gz, rq, yc, bonus, gate, ln_g.reshape(1, -1), ln_b.reshape(1, -1))


def rwkv7(hb, gate, mu, w0, w_up, a0, a_up, k_k, k_a, r_k, ln_g, ln_b, out_dtype):
    prep = rwkv_prep(hb, mu, w0, w_up, a0, a_up, k_k, k_a, r_k)
    intra = rwkv_intra(prep)
    return rwkv_serial(intra, prep[7], gate, ln_g, ln_b, out_dtype)


def kernel(x, norm_g, w_in, w_out, b_mu, b_w0, b_w_up, b_a0, b_a_up, b_k_k, b_k_a, b_r_k,
           b_ln_g, b_ln_b, c_rel_bias, final_g):
    bsz, s, d = x.shape
    assert bsz == 1 and d == D_MODEL
    assert all(s % rows == 0 for rows in (P_TM, B_TI, B_TP, B_TC, A_TB, C_TQ))
    xs = x.reshape(s, d)
    rot_tbl = _rotary_tables(s)
    ret_tbl = _retention_tables()
    bias_tbl = _attn_bias_tables(c_rel_bias)
    xg, ssq = prenorm(xs, norm_g[0])
    for l in range(DEPTH):
        ha, hb, gb, qkv, gc = inproj(xg, ssq, w_in, l)
        ya = retention(ha, rot_tbl, ret_tbl, BF16)
        yb = rwkv7(hb, gb, b_mu[l], b_w0[l], b_w_up[l], b_a0[l], b_a_up[l], b_k_k[l],
                   b_k_a[l], b_r_k[l].reshape(-1), b_ln_g[l], b_ln_b[l], BF16)
        yc = chunk_attention(qkv, gc, bias_tbl, l, BF16)
        if l + 1 < DEPTH:
            xs, xg, ssq = outproj(ya, yb, yc, w_out, l, xs, norm_g[l + 1])
        else:
            xs = outproj(ya, yb, yc, w_out, l, xs)
    return rmsnorm(xs, final_g, F32).reshape(bsz, s, d)
```

```python
import math

import numpy as np
import jax
import jax.numpy as jnp
from jax import lax
from jax.experimental import pallas as pl
from jax.experimental.pallas import tpu as pltpu

F32 = jnp.float32
BF16 = jnp.bfloat16

D_MODEL = 4096
DEPTH = 4
CHUNK = 64
A_W = 1024
A_HEADS = 4
A_DV = 256
A_DK = 128
A_QK = 512
A_IN = 2 * A_QK + 2 * A_W
ROPE_BASE = 10000.0
A_NORM_EPS = 1e-6
B_W = 1024
B_HEAD = 64
B_HEADS = 16
B_RANK = 128
B_SHIFT_W = 3 * B_W + 2 * B_RANK
B_LN_EPS = 64e-5
B_IN = B_SHIFT_W + B_W
C_W = 2048
C_HEADS = 16
C_DH = 128
C_BACK = 8
C_REL_CLIP = 128
RMS_EPS = 1e-6

LANES = 128
SUBLANES = 8
VMEM_LIMIT = 56 * 1024 * 1024

NEG = -1e30


def _params(sem):
    return pltpu.CompilerParams(dimension_semantics=sem, vmem_limit_bytes=VMEM_LIMIT)


def _dot(a, b, prec=None):
    return jnp.dot(a, b, preferred_element_type=F32, precision=prec)


def _dot_nt(a, b, prec=None):
    return lax.dot_general(a, b, (((1,), (1,)), ((), ())), preferred_element_type=F32,
                           precision=prec)


def _dot_tn(a, b, prec=None):
    return lax.dot_general(a, b, (((0,), (0,)), ((), ())), preferred_element_type=F32,
                           precision=prec)


def _silu(g):
    return g * jax.nn.sigmoid(g)


def _rmsnorm_kernel(x_ref, g_ref, o_ref):
    x = x_ref[...]
    ms = jnp.mean(x * x, axis=-1, keepdims=True)
    o_ref[...] = (x * lax.rsqrt(ms + RMS_EPS) * g_ref[...]).astype(o_ref.dtype)


def rmsnorm(x, g, out_dtype, tm=256):
    s, d = x.shape
    return pl.pallas_call(
        _rmsnorm_kernel,
        grid=(s // tm,),
        in_specs=[pl.BlockSpec((tm, d), lambda i: (i, 0)),
                  pl.BlockSpec((1, d), lambda i: (0, 0))],
        out_specs=pl.BlockSpec((tm, d), lambda i: (i, 0)),
        out_shape=jax.ShapeDtypeStruct((s, d), out_dtype),
        compiler_params=_params(("parallel",)),
        name="rmsnorm",
    )(x, g.reshape(1, d))


def _lane_partial_sq(x):
    sq = x * x
    acc = sq[:, 0:LANES]
    for j in range(LANES, x.shape[1], LANES):
        acc = acc + sq[:, j:j + LANES]
    return acc


def _prenorm_kernel(x_ref, g_ref, xg_ref, ssq_ref):
    x = x_ref[...]
    xg_ref[...] = (x * g_ref[...]).astype(xg_ref.dtype)
    ssq_ref[...] = _lane_partial_sq(x)


def prenorm(x, g, tm=256):
    s, d = x.shape
    return pl.pallas_call(
        _prenorm_kernel,
        grid=(s // tm,),
        in_specs=[pl.BlockSpec((tm, d), lambda i: (i, 0)),
                  pl.BlockSpec((1, d), lambda i: (0, 0))],
        out_specs=[pl.BlockSpec((tm, d), lambda i: (i, 0)),
                   pl.BlockSpec((tm, LANES), lambda i: (i, 0))],
        out_shape=[jax.ShapeDtypeStruct((s, d), BF16), jax.ShapeDtypeStruct((s, LANES), F32)],
        compiler_params=_params(("parallel",)),
        name="prenorm",
    )(x, g.reshape(1, d))


P_TM = 2048
P_TN = 256


def _resident_rows(shape):
    return pl.BlockSpec(shape, lambda i, j: (i, 0), pipeline_mode=pl.Buffered(1))


def _proj_segments():
    b0 = A_IN
    g0 = b0 + B_SHIFT_W
    c0 = g0 + B_W
    return ((0, A_IN, 0, None), (b0, B_SHIFT_W, 1, None), (g0, B_W, 2, None),
            (c0, C_W, 3, C_QSCALE), (c0 + C_W, 2 * C_W, 3, None), (c0 + 3 * C_W, C_W, 4, None))


def _inproj_kernel(a_ref, ssq_ref, w_ref, *refs):
    o_refs, r_ref = refs[:-1], refs[-1]
    j = pl.program_id(1)

    @pl.when(j == 0)
    def _():
        ms = jnp.sum(ssq_ref[...], axis=-1, keepdims=True) * (1.0 / D_MODEL)
        r_ref[...] = jnp.broadcast_to(lax.rsqrt(ms + RMS_EPS), r_ref.shape)

    for col0, width, slot, scale in _proj_segments():
        lo, hi = col0 // P_TN, (col0 + width) // P_TN

        @pl.when((j >= lo) & (j < hi))
        def _(slot=slot, scale=scale):
            r = r_ref[...]
            if scale is not None:
                r = r * scale
            acc = _dot(a_ref[...], w_ref[0].astype(BF16))
            acc = acc * jnp.concatenate([r] * (P_TN // LANES), axis=1)
            o_refs[slot][...] = acc.astype(o_refs[slot].dtype)


def inproj(xg, ssq, w_all, layer):
    m, k = xg.shape
    segs = _proj_segments()
    assert m % P_TM == 0 and all(c % P_TN == 0 and w % P_TN == 0 for c, w, _, _ in segs)
    dtypes = (F32, F32, F32, BF16, F32)
    out_specs, out_shape = [], []
    for slot, dt in enumerate(dtypes):
        lo = min(c for c, _, s, _ in segs if s == slot) // P_TN
        n = sum(w for _, w, s, _ in segs if s == slot) // P_TN
        out_specs.append(pl.BlockSpec(
            (P_TM, P_TN), lambda i, j, lo=lo, n=n: (i, jnp.clip(j - lo, 0, n - 1))))
        out_shape.append(jax.ShapeDtypeStruct((m, n * P_TN), dt))
    return pl.pallas_call(
        _inproj_kernel,
        grid=(m // P_TM, w_all.shape[2] // P_TN),
        in_specs=[_resident_rows((P_TM, k)), _resident_rows((P_TM, LANES)),
                  pl.BlockSpec((1, k, P_TN), lambda i, j: (layer, 0, j))],
        out_specs=out_specs,
        out_shape=out_shape,
        scratch_shapes=[pltpu.VMEM((P_TM, LANES), F32)],
        compiler_params=_params(("parallel", "arbitrary")),
        name="inproj",
    )(xg, ssq, w_all)


def _outproj_acc(ya_ref, yb_ref, yc_ref, wa_ref, wb_ref, wc_ref, x_ref):
    acc = _dot(ya_ref[...], wa_ref[0].astype(BF16))
    acc += _dot(yb_ref[...], wb_ref[0].astype(BF16))
    acc += _dot(yc_ref[...], wc_ref[0].astype(BF16))
    return x_ref[...] + acc


def _outproj_kernel(ya_ref, yb_ref, yc_ref, wa_ref, wb_ref, wc_ref, x_ref, o_ref):
    o_ref[...] = _outproj_acc(ya_ref, yb_ref, yc_ref, wa_ref, wb_ref, wc_ref, x_ref)


def _outproj_norm_kernel(ya_ref, yb_ref, yc_ref, wa_ref, wb_ref, wc_ref, x_ref, g_ref,
                         o_ref, xg_ref, ssq_ref):
    j = pl.program_id(1)
    xn = _outproj_acc(ya_ref, yb_ref, yc_ref, wa_ref, wb_ref, wc_ref, x_ref)
    o_ref[...] = xn
    xg_ref[...] = (xn * g_ref[...]).astype(xg_ref.dtype)
    part = _lane_partial_sq(xn)

    @pl.when(j == 0)
    def _():
        ssq_ref[...] = part

    @pl.when(j > 0)
    def _():
        ssq_ref[...] += part


def outproj(ya, yb, yc, w_all, layer, x, g_next=None):
    s = x.shape[0]
    n = w_all.shape[2]
    tm, tn = P_TM, P_TN
    tile = pl.BlockSpec((tm, tn), lambda i, j: (i, j))
    in_specs = [pl.BlockSpec((tm, A_W), lambda i, j: (i, 0)),
                pl.BlockSpec((tm, B_W), lambda i, j: (i, 0)),
                _resident_rows((tm, C_W)),
                pl.BlockSpec((1, A_W, tn), lambda i, j: (layer, 0, j)),
                pl.BlockSpec((1, B_W, tn), lambda i, j: (layer, A_W // B_W, j)),
                pl.BlockSpec((1, C_W, tn), lambda i, j: (layer, (A_W + B_W) // C_W, j)),
                tile]
    if g_next is None:
        return pl.pallas_call(
            _outproj_kernel,
            grid=(s // tm, n // tn),
            in_specs=in_specs,
            out_specs=tile,
            out_shape=jax.ShapeDtypeStruct((s, n), F32),
            compiler_params=_params(("parallel", "parallel")),
            name="outproj",
        )(ya, yb, yc, w_all, w_all, w_all, x)
    return pl.pallas_call(
        _outproj_norm_kernel,
        grid=(s // tm, n // tn),
        in_specs=in_specs + [pl.BlockSpec((1, tn), lambda i, j: (0, j))],
        out_specs=[tile, tile, pl.BlockSpec((tm, LANES), lambda i, j: (i, 0))],
        out_shape=[jax.ShapeDtypeStruct((s, n), F32), jax.ShapeDtypeStruct((s, n), BF16),
                   jax.ShapeDtypeStruct((s, LANES), F32)],
        compiler_params=_params(("parallel", "arbitrary")),
        name="outproj_norm",
    )(ya, yb, yc, w_all, w_all, w_all, x, g_next.reshape(1, n))


C_TQ = 256
C_NKB = C_BACK * CHUNK // C_TQ + 1
C_TK = C_NKB * C_TQ
C_TROW = C_TQ + C_TK


C_HPS = 16
LOG2E = math.log2(math.e)
C_QSCALE = C_DH ** -0.5 * LOG2E


def _attn_kernel(q_ref, k0_ref, k1_ref, k2_ref, v0_ref, v1_ref, v2_ref, g_ref, bias_ref,
                 o_ref):
    heads = [slice(h * C_DH, (h + 1) * C_DH) for h in range(C_HPS)]

    def scores(h):
        c = heads[h]
        kw = jnp.concatenate([k0_ref[:, c], k1_ref[:, c], k2_ref[:, c]], axis=0)
        return _dot_nt(q_ref[:, c], kw) + bias_ref[h]

    s = scores(0)
    for h, c in enumerate(heads):
        s_next = scores(h + 1) if h + 1 < C_HPS else None
        p = jnp.exp2(s - jnp.max(s, axis=-1, keepdims=True))
        l = jnp.sum(p, axis=-1, keepdims=True)
        vw = jnp.concatenate([v0_ref[:, c], v1_ref[:, c], v2_ref[:, c]], axis=0)
        o = _dot(p.astype(BF16), vw)
        o_ref[:, c] = (o / l * _silu(g_ref[:, c].astype(F32))).astype(o_ref.dtype)
        s = s_next


def _bias_kernel(r_ref, o_ref):
    x = jnp.broadcast_to(r_ref[0], (C_TQ, C_TROW))
    y = pltpu.roll(x, 0, 1, stride=1, stride_axis=0)[:, :C_TK]
    qc = lax.broadcasted_iota(jnp.int32, (C_TQ, C_TK), 0) // CHUNK
    kcol = lax.broadcasted_iota(jnp.int32, (C_TQ, C_TK), 1)
    dchunk = qc - (kcol // CHUNK - (C_TK - C_TQ) // CHUNK)
    band = (dchunk >= 0) & (dchunk <= C_BACK)
    for v in range(C_NKB):
        o_ref[v, 0] = jnp.where(band & (kcol >= (C_NKB - 1 - v) * C_TQ), y, NEG)


def _attn_bias_tables(rel_bias):
    j = np.arange(C_TROW)
    dist = (C_TK - j) % C_TROW - C_TQ
    idx = np.clip(dist, -(CHUNK - 1), C_REL_CLIP) + (CHUNK - 1)
    nl, nh = rel_bias.shape[0], rel_bias.shape[1]
    row = jnp.take(rel_bias, jnp.asarray(idx, jnp.int32), axis=-1) * LOG2E
    return pl.pallas_call(
        _bias_kernel,
        grid=(nl, nh),
        in_specs=[pl.BlockSpec((1, 1, C_TROW), lambda l, h: (l * nh + h, 0, 0))],
        out_specs=pl.BlockSpec((C_NKB, 1, C_TQ, C_TK), lambda l, h: (l, h, 0, 0)),
        out_shape=jax.ShapeDtypeStruct((nl * C_NKB, nh, C_TQ, C_TK), F32),
        compiler_params=_params(("parallel", "parallel")),
        name="attn_bias",
    )(row.reshape(nl * nh, 1, C_TROW))


def chunk_attention(qkv, gate, bias_tbl, layer, out_dtype):
    s = qkv.shape[0]
    nq = s // C_TQ
    wblk = C_HPS * C_DH
    ng = C_HEADS // C_HPS
    assert C_NKB == 3

    def kv_spec(base, back):
        return pl.BlockSpec((C_TQ, wblk),
                            lambda h, i: (jnp.maximum(i - back, 0), base + h))

    def bias_map(h, i):
        return (layer * C_NKB + jnp.minimum(i, C_NKB - 1), h, 0, 0)

    return pl.pallas_call(
        _attn_kernel,
        grid=(ng, nq),
        in_specs=[pl.BlockSpec((C_TQ, wblk), lambda h, i: (i, h)),
                  kv_spec(ng, 2), kv_spec(ng, 1), kv_spec(ng, 0),
                  kv_spec(2 * ng, 2), kv_spec(2 * ng, 1), kv_spec(2 * ng, 0),
                  pl.BlockSpec((C_TQ, wblk), lambda h, i: (i, h)),
                  pl.BlockSpec((None, C_HPS, C_TQ, C_TK), bias_map)],
        out_specs=pl.BlockSpec((C_TQ, wblk), lambda h, i: (i, h)),
        out_shape=jax.ShapeDtypeStruct((s, C_W), out_dtype),
        compiler_params=_params(("parallel", "parallel")),
        name="chunk_attn",
    )(qkv, qkv, qkv, qkv, qkv, qkv, qkv, gate, bias_tbl)


A_TB = 256


def _swap_pairs(x):
    lane = lax.broadcasted_iota(jnp.int32, x.shape, x.ndim - 1)
    nxt = pltpu.roll(x, LANES - 1, x.ndim - 1)
    prv = pltpu.roll(x, 1, x.ndim - 1)
    return jnp.where(lane % 2 == 0, nxt, prv)


def _retention_kernel(q_ref, k_ref, v_ref, g_ref, cos_ref, sin_ref, mask_ref, dq_ref,
                      dk_ref, gb_ref, o_ref, st_ref):
    i = pl.program_id(0)

    @pl.when(i == 0)
    def _():
        st_ref[...] = jnp.zeros_like(st_ref)

    cos = cos_ref[...]
    sin = sin_ref[...]
    hs = range(A_HEADS)

    def rot(ref, h):
        x = ref[:, h * A_DK:(h + 1) * A_DK].astype(F32)
        return x * cos + _swap_pairs(x) * sin

    q = [rot(q_ref, h) for h in hs]
    k = [rot(k_ref, h) * (A_DK ** -0.5) for h in hs]
    v = [v_ref[:, h * A_DV:(h + 1) * A_DV].astype(BF16) for h in hs]
    st = [st_ref[h] for h in hs]
    sc = [_dot_nt(q[h].astype(BF16), k[h].astype(BF16)) * mask_ref[h] for h in hs]
    oi = [_dot((q[h] * dq_ref[h]).astype(BF16), st[h].astype(BF16)) for h in hs]
    o = [_dot(sc[h].astype(BF16), v[h]) + oi[h] for h in hs]
    kv = [_dot_tn((k[h] * dk_ref[h]).astype(BF16), v[h]) for h in hs]
    for h in hs:
        st_ref[h] = gb_ref[h] * st[h] + kv[h]
        on = o[h] * lax.rsqrt(jnp.mean(o[h] * o[h], axis=-1, keepdims=True) + A_NORM_EPS)
        cols = slice(h * A_DV, (h + 1) * A_DV)
        o_ref[:, cols] = (on * _silu(g_ref[:, cols].astype(F32))).astype(o_ref.dtype)


def _retention_tables():
    h = np.arange(A_HEADS, dtype=np.float64)
    log_g = np.log(1.0 - 2.0 ** (-5.0 - h))
    pos = np.arange(A_TB)
    d = pos[:, None] - pos[None, :]
    cq = pos[:, None] // CHUNK
    ck = pos[None, :] // CHUNK
    same = cq == ck
    past = ck < cq
    expo = np.where(same, np.abs(d), np.where(past, d, 0)).astype(np.float64)
    mask = np.exp(log_g[:, None, None] * expo[None]) * (same | past)[None]
    dq = np.exp(log_g[:, None] * (pos + 1.0)[None, :])
    dk = np.exp(log_g[:, None] * (A_TB - 1.0 - pos)[None, :])
    gb = np.exp(log_g * A_TB)
    ones = np.ones((1, 1, LANES))
    return (jnp.asarray(mask, F32), jnp.asarray(dq[:, :, None] * ones, F32),
            jnp.asarray(dk[:, :, None] * ones, F32),
            jnp.asarray(gb[:, None, None] * np.ones((1, 1, A_DV)), F32))


def _rotary_tables(s):
    inv = 1.0 / (ROPE_BASE ** (jnp.arange(0, A_DK, 2, dtype=F32) / A_DK))
    ang = jnp.arange(s).astype(F32)[:, None] * inv[None, :]
    cos = jnp.repeat(jnp.cos(ang), 2, axis=-1)
    sin = jnp.repeat(jnp.sin(ang), 2, axis=-1)
    sign = jnp.asarray(np.tile(np.array([-1.0, 1.0]), A_DK // 2), F32)
    return cos, sin * sign[None, :]


def retention(ha, rot_tbl, ret_tbl, out_dtype):
    s = ha.shape[0]
    cos, sin = rot_tbl
    mask, dq, dk, gb = ret_tbl
    whole = lambda shape: pl.BlockSpec(shape, lambda i: (0,) * len(shape))
    return pl.pallas_call(
        _retention_kernel,
        grid=(s // A_TB,),
        in_specs=[pl.BlockSpec((A_TB, A_QK), lambda i: (i, 0)),
                  pl.BlockSpec((A_TB, A_QK), lambda i: (i, 1)),
                  pl.BlockSpec((A_TB, A_W), lambda i: (i, 2 * A_QK // A_W)),
                  pl.BlockSpec((A_TB, A_W), lambda i: (i, 2 * A_QK // A_W + 1)),
                  pl.BlockSpec((A_TB, A_DK), lambda i: (i, 0)),
                  pl.BlockSpec((A_TB, A_DK), lambda i: (i, 0)),
                  whole((A_HEADS, A_TB, A_TB)), whole((A_HEADS, A_TB, LANES)),
                  whole((A_HEADS, A_TB, LANES)), whole((A_HEADS, 1, A_DV))],
        out_specs=pl.BlockSpec((A_TB, A_W), lambda i: (i, 0)),
        out_shape=jax.ShapeDtypeStruct((s, A_W), out_dtype),
        scratch_shapes=[pltpu.VMEM((A_HEADS, A_DK, A_DV), F32)],
        compiler_params=_params(("arbitrary",)),
        name="retention",
    )(ha, ha, ha, ha, cos, sin, mask, dq, dk, gb)


B_TP = 512
B_TI = 2048
B_NCI = B_TI // CHUNK
B_TC = 512
B_NCB = B_TC // CHUNK
B_NPAIR = B_W // LANES
B_INV_BASE = 16


def _seg_ones():
    r = lax.broadcasted_iota(jnp.int32, (LANES, LANES), 0) // B_HEAD
    c = lax.broadcasted_iota(jnp.int32, (LANES, LANES), 1) // B_HEAD
    return (r == c).astype(BF16)


def _seg_sum(x):
    ones = _seg_ones()
    hi = x.astype(BF16)
    lo = (x - hi.astype(F32)).astype(BF16)
    parts = [_dot(hi[:, j:j + LANES], ones) + _dot(lo[:, j:j + LANES], ones)
             for j in range(0, x.shape[1], LANES)]
    return jnp.concatenate(parts, axis=1)


def _rwkv_prep_kernel(hb_ref, prev_ref, mu_ref, w0_ref, wup_ref, a0_ref, aup_ref, kk_ref,
                      ka_ref, rk_ref, rt_ref, kq_ref, bt_ref, kt_ref, v_ref, bte_ref,
                      kte_ref, bonus_ref, pl_ref):
    i = pl.program_id(0)
    hb = hb_ref[...]
    prev_row = jnp.where(i == 0, 0.0, prev_ref[SUBLANES - 1:SUBLANES, :])
    rolled = pltpu.roll(hb, 1, 0)
    row = lax.broadcasted_iota(jnp.int32, (SUBLANES, hb.shape[1]), 0)
    shifted = jnp.concatenate([jnp.where(row == 0, prev_row, rolled[0:SUBLANES]),
                               rolled[SUBLANES:]], axis=0)
    xs = hb + (shifted - hb) * mu_ref[...]
    r = xs[:, 0:B_W]
    k = xs[:, B_W:2 * B_W]
    v = xs[:, 2 * B_W:3 * B_W]
    wd = xs[:, 3 * B_W:3 * B_W + B_RANK]
    ad = xs[:, 3 * B_W + B_RANK:3 * B_W + 2 * B_RANK]
    z = w0_ref[...] + _mm(jnp.tanh(wd), wup_ref[...])
    lw = -math.exp(-0.5) * jax.nn.sigmoid(z)
    a = jax.nn.sigmoid(a0_ref[...] + _mm(ad, aup_ref[...]))
    kk = k * kk_ref[...]
    kk = kk * lax.rsqrt(jnp.maximum(_seg_sum(kk * kk), 1e-24))
    kmod = k * (1.0 + (a - 1.0) * ka_ref[...])
    bonus_ref[...] = _seg_sum(r * kmod * rk_ref[...]) * v
    tr = lax.broadcasted_iota(jnp.int32, (B_TP, B_TP), 0)
    tc = lax.broadcasted_iota(jnp.int32, (B_TP, B_TP), 1)
    tri = ((tr // CHUNK == tc // CHUNK) & (tr >= tc)).astype(BF16)
    hi = lw.astype(BF16)
    rest = lw - hi.astype(F32)
    mid = rest.astype(BF16)
    lo = (rest - mid.astype(F32)).astype(BF16)
    cum = _dot(tri, hi) + _dot(tri, mid) + _dot(tri, lo)
    nc = B_TP // CHUNK
    ends = [cum[(c + 1) * CHUNK - 1:(c + 1) * CHUNK, :] for c in range(nc)]
    p_inv = jnp.exp(-cum)
    p_end = jnp.concatenate([jnp.exp(ends[c] - cum[c * CHUNK:(c + 1) * CHUNK, :])
                             for c in range(nc)], axis=0)
    rt_ref[...] = (r * jnp.exp(cum)).astype(BF16)
    kq_ref[...] = (kk * jnp.exp(cum - lw)).astype(BF16)
    bt_ref[...] = (kk * a * p_inv).astype(BF16)
    kt_ref[...] = (kmod * p_inv).astype(BF16)
    v_ref[...] = v.astype(BF16)
    bte_ref[...] = (kk * a * p_end).astype(BF16)
    kte_ref[...] = (kmod * p_end).astype(BF16)
    pl_ref[...] = jnp.exp(jnp.concatenate(ends, axis=0))


def rwkv_prep(hb, mu, w0, w_up, a0, a_up, k_k, k_a, r_k):
    s = hb.shape[0]
    nb = s // B_TP
    vec = lambda n: pl.BlockSpec((1, n), lambda i: (0, 0))
    full = lambda a, b: pl.BlockSpec((a, b), lambda i: (0, 0))
    big = pl.BlockSpec((B_TP, B_W), lambda i: (i, 0))
    outs = ([jax.ShapeDtypeStruct((s, B_W), BF16)] * 7
            + [jax.ShapeDtypeStruct((s, B_W), F32),
               jax.ShapeDtypeStruct((s // CHUNK, B_W), F32)])
    return pl.pallas_call(
        _rwkv_prep_kernel,
        grid=(nb,),
        in_specs=[pl.BlockSpec((B_TP, B_SHIFT_W), lambda i: (i, 0)),
                  pl.BlockSpec((SUBLANES, B_SHIFT_W),
                               lambda i: (jnp.maximum(i * (B_TP // SUBLANES) - 1, 0), 0)),
                  vec(B_SHIFT_W), vec(B_W), full(B_RANK, B_W), vec(B_W), full(B_RANK, B_W),
                  vec(B_W), vec(B_W), vec(B_W)],
        out_specs=[big] * 8 + [pl.BlockSpec((B_TP // CHUNK, B_W), lambda i: (i, 0))],
        out_shape=outs,
        compiler_params=_params(("parallel",)),
        name="rwkv_prep",
    )(hb, hb, mu.reshape(1, -1), w0.reshape(1, -1), w_up, a0.reshape(1, -1), a_up,
      k_k.reshape(1, -1), k_a.reshape(1, -1), r_k.reshape(1, -1))


def _stack_heads(x):
    lane = lax.broadcasted_iota(jnp.int32, x.shape, 1)
    zero = jnp.zeros_like(x)
    return jnp.concatenate([jnp.where(lane < B_HEAD, x, zero),
                            jnp.where(lane >= B_HEAD, x, zero)], axis=0)


def _fold_heads(x):
    return x[:CHUNK] + x[CHUNK:]


def _mm(a, b):
    return _dot(a.astype(BF16), b.astype(BF16))


def _rwkv_intra_kernel(rt_ref, kq_ref, bt_ref, kt_ref, v_ref, bte_ref, kte_ref, pl_ref,
                       mz_ref, gz_ref, rq_ref, yc_ref):
    n2 = 2 * CHUNK
    r_i = lax.broadcasted_iota(jnp.int32, (n2, n2), 0)
    c_i = lax.broadcasted_iota(jnp.int32, (n2, n2), 1)
    same_head = (r_i // CHUNK) == (c_i // CHUNK)
    strict = (same_head & (r_i > c_i)).astype(F32)
    incl = (same_head & (r_i >= c_i)).astype(F32)
    eye = (r_i == c_i).astype(F32)
    bd16 = ((r_i // B_INV_BASE) == (c_i // B_INV_BASE)).astype(F32)
    bd32 = ((r_i // (2 * B_INV_BASE)) == (c_i // (2 * B_INV_BASE))).astype(F32)

    cs = range(B_NCI)

    def load(ref):
        return [_stack_heads(ref[c * CHUNK:(c + 1) * CHUNK, :]) for c in cs]

    def each(fn, *lists):
        return [fn(*args) for args in zip(*lists)]

    rt, kq, bt, kt, vv, bte, kte = (load(r) for r in (rt_ref, kq_ref, bt_ref, kt_ref, v_ref,
                                                      bte_ref, kte_ref))
    ab = each(lambda kq_, rt_, bt_, kt_: _dot_nt(jnp.concatenate([kq_, rt_], axis=0),
                                                 jnp.concatenate([bt_, kt_], axis=0)),
              kq, rt, bt, kt)
    a_m = [x[:n2, :n2] * strict for x in ab]
    b_m = [x[:n2, n2:] * strict for x in ab]
    aq = [x[n2:, :n2] * incl for x in ab]
    bq = [x[n2:, n2:] * incl for x in ab]
    nn = [-(x * bd16) for x in a_m]
    t = [eye + x for x in nn]
    sq = each(_mm, nn, nn)
    doublings = B_INV_BASE.bit_length() - 2
    for step in range(doublings):
        t = each(lambda t_, s_: t_ + _mm(t_, s_), t, sq)
        if step + 1 < doublings:
            sq = each(_mm, sq, sq)
    for blk in (bd32 - bd16, 1.0 - bd32):
        e = each(lambda a_, t_: _mm(a_ * blk, t_), a_m, t)
        t = each(lambda t_, e_: t_ - _mm(t_, e_), t, e)
    bv = each(_mm, b_m, vv)
    tw = each(lambda t_, kq_, bv_: _mm(t_, jnp.concatenate([kq_.astype(F32), bv_], axis=1)),
              t, kq, bv)
    w = [x[:, :n2].astype(BF16) for x in tw]
    u0 = [(-x[:, n2:]).astype(BF16) for x in tw]
    aw = each(lambda aq_, w_, u_: _mm(aq_, jnp.concatenate([w_, u_], axis=1)), aq, w, u0)
    bqv = each(_mm, bq, vv)
    wb = each(_dot_tn, w, bte)
    gz = each(lambda u_, v_, b_, k_: _dot_tn(jnp.concatenate([u_, v_], axis=0),
                                             jnp.concatenate([b_, k_], axis=0)),
              u0, vv, bte, kte)
    for c in cs:
        rq_ref[0, c] = _fold_heads(rt[c].astype(F32) - aw[c][:, :n2]).astype(BF16)
        yc_ref[c * CHUNK:(c + 1) * CHUNK, :] = _fold_heads(bqv[c] + aw[c][:, n2:])
        mz_ref[0, c] = _fold_heads(eye * pl_ref[c:c + 1, :] - wb[c]).astype(BF16)
        gz_ref[0, c] = _fold_heads(gz[c])


def rwkv_intra(prep):
    rt, kq, bt, kt, v, bte, kte, _, p_end = prep
    s = rt.shape[0]
    nchunk = s // CHUNK
    blk = pl.BlockSpec((B_TI, LANES), lambda p, i: (i, p))
    mat = pl.BlockSpec((1, B_NCI, CHUNK, LANES), lambda p, i: (p, i, 0, 0))
    mshape = (B_NPAIR, nchunk, CHUNK, LANES)
    return pl.pallas_call(
        _rwkv_intra_kernel,
        grid=(B_NPAIR, s // B_TI),
        in_specs=[blk] * 7 + [pl.BlockSpec((B_NCI, LANES), lambda p, i: (i, p))],
        out_specs=[mat, mat, mat, blk],
        out_shape=[jax.ShapeDtypeStruct(mshape, BF16), jax.ShapeDtypeStruct(mshape, F32),
                   jax.ShapeDtypeStruct(mshape, BF16), jax.ShapeDtypeStruct((s, B_W), F32)],
        compiler_params=_params(("parallel", "parallel")),
        name="rwkv_intra",
    )(rt, kq, bt, kt, v, bte, kte, p_end)


def _rwkv_serial_kernel(mz_ref, gz_ref, rq_ref, yc_ref, bonus_ref, g_ref, lng_ref, lnb_ref,
                        o_ref, zt_ref, y_ref):
    i = pl.program_id(0)

    @pl.when(i == 0)
    def _():
        zt_ref[...] = jnp.zeros_like(zt_ref)

    zt = [zt_ref[p] for p in range(B_NPAIR)]
    for c in range(B_NCB):
        rows = slice(c * CHUNK, (c + 1) * CHUNK)
        zb = [z.astype(BF16) for z in zt]
        y2 = [_dot_nt(_stack_heads(rq_ref[p, c]), zb[p]) for p in range(B_NPAIR)]
        zt = [_dot(zb[p], _stack_heads(mz_ref[p, c])) + _stack_heads(gz_ref[p, c])
              for p in range(B_NPAIR)]
        for p in range(B_NPAIR):
            cols = slice(p * LANES, (p + 1) * LANES)
            y_ref[rows, cols] = y2[p][:CHUNK] + y2[p][CHUNK:] + yc_ref[rows, cols]
    for p in range(B_NPAIR):
        zt_ref[p] = zt[p]
    y = y_ref[...]
    mean = _seg_sum(y) * (1.0 / B_HEAD)
    yc = y - mean
    var = _seg_sum(yc * yc) * (1.0 / B_HEAD)
    yn = yc * lax.rsqrt(var + B_LN_EPS) * lng_ref[...] + lnb_ref[...]
    o_ref[...] = ((yn + bonus_ref[...]) * _silu(g_ref[...].astype(F32))).astype(o_ref.dtype)


def rwkv_serial(intra, bonus, gate, ln_g, ln_b, out_dtype):
    mz, gz, rq, yc = intra
    s = yc.shape[0]
    mat = pl.BlockSpec((B_NPAIR, B_NCB, CHUNK, LANES), lambda i: (0, i, 0, 0))
    blk = pl.BlockSpec((B_TC, B_W), lambda i: (i, 0))
    vec = pl.BlockSpec((1, B_W), lambda i: (0, 0))
    return pl.pallas_call(
        _rwkv_serial_kernel,
        grid=(s // B_TC,),
        in_specs=[mat, mat, mat, blk, blk, blk, vec, vec],
        out_specs=blk,
        out_shape=jax.ShapeDtypeStruct((s, B_W), out_dtype),
        scratch_shapes=[pltpu.VMEM((B_NPAIR, LANES, LANES), F32),
                        pltpu.VMEM((B_TC, B_W), F32)],
        compiler_params=_params(("arbitrary",)),
        name="rwkv_serial",
    )(mz, gz, rq, yc, bonus, gate, ln_g.reshape(1, -1), ln_b.reshape(1, -1))


def rwkv7(hb, gate, mu, w0, w_up, a0, a_up, k_k, k_a, r_k, ln_g, ln_b, out_dtype):
    prep = rwkv_prep(hb, mu, w0, w_up, a0, a_up, k_k, k_a, r_k)
    intra = rwkv_intra(prep)
    return rwkv_serial(intra, prep[7], gate, ln_g, ln_b, out_dtype)


def kernel(x, norm_g, w_in, w_out, b_mu, b_w0, b_w_up, b_a0, b_a_up, b_k_k, b_k_a, b_r_k,
           b_ln_g, b_ln_b, c_rel_bias, final_g):
    bsz, s, d = x.shape
    assert bsz == 1 and d == D_MODEL
    assert all(s % rows == 0 for rows in (P_TM, B_TI, B_TP, B_TC, A_TB, C_TQ))
    xs = x.reshape(s, d)
    rot_tbl = _rotary_tables(s)
    ret_tbl = _retention_tables()
    bias_tbl = _attn_bias_tables(c_rel_bias)
    xg, ssq = prenorm(xs, norm_g[0])
    for l in range(DEPTH):
        ha, hb, gb, qkv, gc = inproj(xg, ssq, w_in, l)
        ya = retention(ha, rot_tbl, ret_tbl, BF16)
        yb = rwkv7(hb, gb, b_mu[l], b_w0[l], b_w_up[l], b_a0[l], b_a_up[l], b_k_k[l],
                   b_k_a[l], b_r_k[l].reshape(-1), b_ln_g[l], b_ln_b[l], BF16)
        yc = chunk_attention(qkv, gc, bias_tbl, l, BF16)
        if l + 1 < DEPTH:
            xs, xg, ssq = outproj(ya, yb, yc, w_out, l, xs, norm_g[l + 1])
        else:
            xs = outproj(ya, yb, yc, w_out, l, xs)
    return rmsnorm(xs, final_g, F32).reshape(bsz, s, d)
```

```python
import math

import numpy as np
import jax
import jax.numpy as jnp
from jax import lax
from jax.experimental import pallas as pl
from jax.experimental.pallas import tpu as pltpu

F32 = jnp.float32
BF16 = jnp.bfloat16

D_MODEL = 4096
DEPTH = 4
CHUNK = 64
A_W = 1024
A_HEADS = 4
A_DV = 256
A_DK = 128
A_QK = 512
A_IN = 2 * A_QK + 2 * A_W
ROPE_BASE = 10000.0
A_NORM_EPS = 1e-6
B_W = 1024
B_HEAD = 64
B_HEADS = 16
B_RANK = 128
B_SHIFT_W = 3 * B_W + 2 * B_RANK
B_LN_EPS = 64e-5
B_IN = B_SHIFT_W + B_W
C_W = 2048
C_HEADS = 16
C_DH = 128
C_BACK = 8
C_REL_CLIP = 128
RMS_EPS = 1e-6

LANES = 128
SUBLANES = 8
VMEM_LIMIT = 56 * 1024 * 1024

NEG = -1e30


def _params(sem):
    return pltpu.CompilerParams(dimension_semantics=sem, vmem_limit_bytes=VMEM_LIMIT)


def _dot(a, b, prec=None):
    return jnp.dot(a, b, preferred_element_type=F32, precision=prec)


def _dot_nt(a, b, prec=None):
    return lax.dot_general(a, b, (((1,), (1,)), ((), ())), preferred_element_type=F32,
                           precision=prec)


def _dot_tn(a, b, prec=None):
    return lax.dot_general(a, b, (((0,), (0,)), ((), ())), preferred_element_type=F32,
                           precision=prec)


def _silu(g):
    return g * jax.nn.sigmoid(g)


def _rmsnorm_kernel(x_ref, g_ref, o_ref):
    x = x_ref[...]
    ms = jnp.mean(x * x, axis=-1, keepdims=True)
    o_ref[...] = (x * lax.rsqrt(ms + RMS_EPS) * g_ref[...]).astype(o_ref.dtype)


def rmsnorm(x, g, out_dtype, tm=256):
    s, d = x.shape
    return pl.pallas_call(
        _rmsnorm_kernel,
        grid=(s // tm,),
        in_specs=[pl.BlockSpec((tm, d), lambda i: (i, 0)),
                  pl.BlockSpec((1, d), lambda i: (0, 0))],
        out_specs=pl.BlockSpec((tm, d), lambda i: (i, 0)),
        out_shape=jax.ShapeDtypeStruct((s, d), out_dtype),
        compiler_params=_params(("parallel",)),
        name="rmsnorm",
    )(x, g.reshape(1, d))


def _lane_partial_sq(x):
    sq = x * x
    acc = sq[:, 0:LANES]
    for j in range(LANES, x.shape[1], LANES):
        acc = acc + sq[:, j:j + LANES]
    return acc


def _prenorm_kernel(x_ref, g_ref, xg_ref, ssq_ref):
    x = x_ref[...]
    xg_ref[...] = (x * g_ref[...]).astype(xg_ref.dtype)
    ssq_ref[...] = _lane_partial_sq(x)


def prenorm(x, g, tm=256):
    s, d = x.shape
    return pl.pallas_call(
        _prenorm_kernel,
        grid=(s // tm,),
        in_specs=[pl.BlockSpec((tm, d), lambda i: (i, 0)),
                  pl.BlockSpec((1, d), lambda i: (0, 0))],
        out_specs=[pl.BlockSpec((tm, d), lambda i: (i, 0)),
                   pl.BlockSpec((tm, LANES), lambda i: (i, 0))],
        out_shape=[jax.ShapeDtypeStruct((s, d), BF16), jax.ShapeDtypeStruct((s, LANES), F32)],
        compiler_params=_params(("parallel",)),
        name="prenorm",
    )(x, g.reshape(1, d))


P_TM = 2048
P_TN = 256


def _resident_rows(shape):
    return pl.BlockSpec(shape, lambda i, j: (i, 0), pipeline_mode=pl.Buffered(1))


def _proj_segments():
    b0 = A_IN
    g0 = b0 + B_SHIFT_W
    c0 = g0 + B_W
    return ((0, A_IN, 0, None), (b0, B_SHIFT_W, 1, None), (g0, B_W, 2, None),
            (c0, C_W, 3, C_QSCALE), (c0 + C_W, 2 * C_W, 3, None), (c0 + 3 * C_W, C_W, 4, None))


def _inproj_kernel(a_ref, ssq_ref, w_ref, *refs):
    o_refs, r_ref = refs[:-1], refs[-1]
    j = pl.program_id(1)

    @pl.when(j == 0)
    def _():
        ms = jnp.sum(ssq_ref[...], axis=-1, keepdims=True) * (1.0 / D_MODEL)
        r_ref[...] = jnp.broadcast_to(lax.rsqrt(ms + RMS_EPS), r_ref.shape)

    for col0, width, slot, scale in _proj_segments():
        lo, hi = col0 // P_TN, (col0 + width) // P_TN

        @pl.when((j >= lo) & (j < hi))
        def _(slot=slot, scale=scale):
            r = r_ref[...]
            if scale is not None:
                r = r * scale
            acc = _dot(a_ref[...], w_ref[0].astype(BF16))
            acc = acc * jnp.concatenate([r] * (P_TN // LANES), axis=1)
            o_refs[slot][...] = acc.astype(o_refs[slot].dtype)


def inproj(xg, ssq, w_all, layer):
    m, k = xg.shape
    segs = _proj_segments()
    assert m % P_TM == 0 and all(c % P_TN == 0 and w % P_TN == 0 for c, w, _, _ in segs)
    dtypes = (F32, F32, F32, BF16, F32)
    out_specs, out_shape = [], []
    for slot, dt in enumerate(dtypes):
        lo = min(c for c, _, s, _ in segs if s == slot) // P_TN
        n = sum(w for _, w, s, _ in segs if s == slot) // P_TN
        out_specs.append(pl.BlockSpec(
            (P_TM, P_TN), lambda i, j, lo=lo, n=n: (i, jnp.clip(j - lo, 0, n - 1))))
        out_shape.append(jax.ShapeDtypeStruct((m, n * P_TN), dt))
    return pl.pallas_call(
        _inproj_kernel,
        grid=(m // P_TM, w_all.shape[2] // P_TN),
        in_specs=[_resident_rows((P_TM, k)), _resident_rows((P_TM, LANES)),
                  pl.BlockSpec((1, k, P_TN), lambda i, j: (layer, 0, j))],
        out_specs=out_specs,
        out_shape=out_shape,
        scratch_shapes=[pltpu.VMEM((P_TM, LANES), F32)],
        compiler_params=_params(("parallel", "arbitrary")),
        name="inproj",
    )(xg, ssq, w_all)


def _outproj_acc(ya_ref, yb_ref, yc_ref, wa_ref, wb_ref, wc_ref, x_ref):
    acc = _dot(ya_ref[...], wa_ref[0].astype(BF16))
    acc += _dot(yb_ref[...], wb_ref[0].astype(BF16))
    acc += _dot(yc_ref[...], wc_ref[0].astype(BF16))
    return x_ref[...] + acc


def _outproj_kernel(ya_ref, yb_ref, yc_ref, wa_ref, wb_ref, wc_ref, x_ref, o_ref):
    o_ref[...] = _outproj_acc(ya_ref, yb_ref, yc_ref, wa_ref, wb_ref, wc_ref, x_ref)


def _outproj_norm_kernel(ya_ref, yb_ref, yc_ref, wa_ref, wb_ref, wc_ref, x_ref, g_ref,
                         o_ref, xg_ref, ssq_ref):
    j = pl.program_id(1)
    xn = _outproj_acc(ya_ref, yb_ref, yc_ref, wa_ref, wb_ref, wc_ref, x_ref)
    o_ref[...] = xn
    xg_ref[...] = (xn * g_ref[...]).astype(xg_ref.dtype)
    part = _lane_partial_sq(xn)

    @pl.when(j == 0)
    def _():
        ssq_ref[...] = part

    @pl.when(j > 0)
    def _():
        ssq_ref[...] += part


def outproj(ya, yb, yc, w_all, layer, x, g_next=None):
    s = x.shape[0]
    n = w_all.shape[2]
    tm, tn = P_TM, P_TN
    tile = pl.BlockSpec((tm, tn), lambda i, j: (i, j))
    in_specs = [pl.BlockSpec((tm, A_W), lambda i, j: (i, 0)),
                pl.BlockSpec((tm, B_W), lambda i, j: (i, 0)),
                _resident_rows((tm, C_W)),
                pl.BlockSpec((1, A_W, tn), lambda i, j: (layer, 0, j)),
                pl.BlockSpec((1, B_W, tn), lambda i, j: (layer, A_W // B_W, j)),
                pl.BlockSpec((1, C_W, tn), lambda i, j: (layer, (A_W + B_W) // C_W, j)),
                tile]
    if g_next is None:
        return pl.pallas_call(
            _outproj_kernel,
            grid=(s // tm, n // tn),
            in_specs=in_specs,
            out_specs=tile,
            out_shape=jax.ShapeDtypeStruct((s, n), F32),
            compiler_params=_params(("parallel", "parallel")),
            name="outproj",
        )(ya, yb, yc, w_all, w_all, w_all, x)
    return pl.pallas_call(
        _outproj_norm_kernel,
        grid=(s // tm, n // tn),
        in_specs=in_specs + [pl.BlockSpec((1, tn), lambda i, j: (0, j))],
        out_specs=[tile, tile, pl.BlockSpec((tm, LANES), lambda i, j: (i, 0))],
        out_shape=[jax.ShapeDtypeStruct((s, n), F32), jax.ShapeDtypeStruct((s, n), BF16),
                   jax.ShapeDtypeStruct((s, LANES), F32)],
        compiler_params=_params(("parallel", "arbitrary")),
        name="outproj_norm",
    )(ya, yb, yc, w_all, w_all, w_all, x, g_next.reshape(1, n))


C_TQ = 256
C_NKB = C_BACK * CHUNK // C_TQ + 1
C_TK = C_NKB * C_TQ
C_TROW = C_TQ + C_TK


C_HPS = 16
LOG2E = math.log2(math.e)
C_QSCALE = C_DH ** -0.5 * LOG2E


def _attn_kernel(q_ref, k0_ref, k1_ref, k2_ref, v0_ref, v1_ref, v2_ref, g_ref, bias_ref,
                 o_ref):
    heads = [slice(h * C_DH, (h + 1) * C_DH) for h in range(C_HPS)]

    def scores(h):
        c = heads[h]
        kw = jnp.concatenate([k0_ref[:, c], k1_ref[:, c], k2_ref[:, c]], axis=0)
        return _dot_nt(q_ref[:, c], kw) + bias_ref[h]

    s = scores(0)
    for h, c in enumerate(heads):
        s_next = scores(h + 1) if h + 1 < C_HPS else None
        p = jnp.exp2(s - jnp.max(s, axis=-1, keepdims=True))
        l = jnp.sum(p, axis=-1, keepdims=True)
        vw = jnp.concatenate([v0_ref[:, c], v1_ref[:, c], v2_ref[:, c]], axis=0)
        o = _dot(p.astype(BF16), vw)
        o_ref[:, c] = (o / l * _silu(g_ref[:, c].astype(F32))).astype(o_ref.dtype)
        s = s_next


def _bias_kernel(r_ref, o_ref):
    x = jnp.broadcast_to(r_ref[0], (C_TQ, C_TROW))
    y = pltpu.roll(x, 0, 1, stride=1, stride_axis=0)[:, :C_TK]
    qc = lax.broadcasted_iota(jnp.int32, (C_TQ, C_TK), 0) // CHUNK
    kcol = lax.broadcasted_iota(jnp.int32, (C_TQ, C_TK), 1)
    dchunk = qc - (kcol // CHUNK - (C_TK - C_TQ) // CHUNK)
    band = (dchunk >= 0) & (dchunk <= C_BACK)
    for v in range(C_NKB):
        o_ref[v, 0] = jnp.where(band & (kcol >= (C_NKB - 1 - v) * C_TQ), y, NEG)


def _attn_bias_tables(rel_bias):
    j = np.arange(C_TROW)
    dist = (C_TK - j) % C_TROW - C_TQ
    idx = np.clip(dist, -(CHUNK - 1), C_REL_CLIP) + (CHUNK - 1)
    nl, nh = rel_bias.shape[0], rel_bias.shape[1]
    row = jnp.take(rel_bias, jnp.asarray(idx, jnp.int32), axis=-1) * LOG2E
    return pl.pallas_call(
        _bias_kernel,
        grid=(nl, nh),
        in_specs=[pl.BlockSpec((1, 1, C_TROW), lambda l, h: (l * nh + h, 0, 0))],
        out_specs=pl.BlockSpec((C_NKB, 1, C_TQ, C_TK), lambda l, h: (l, h, 0, 0)),
        out_shape=jax.ShapeDtypeStruct((nl * C_NKB, nh, C_TQ, C_TK), F32),
        compiler_params=_params(("parallel", "parallel")),
        name="attn_bias",
    )(row.reshape(nl * nh, 1, C_TROW))


def chunk_attention(qkv, gate, bias_tbl, layer, out_dtype):
    s = qkv.shape[0]
    nq = s // C_TQ
    wblk = C_HPS * C_DH
    ng = C_HEADS // C_HPS
    assert C_NKB == 3

    def kv_spec(base, back):
        return pl.BlockSpec((C_TQ, wblk),
                            lambda h, i: (jnp.maximum(i - back, 0), base + h))

    def bias_map(h, i):
        return (layer * C_NKB + jnp.minimum(i, C_NKB - 1), h, 0, 0)

    return pl.pallas_call(
        _attn_kernel,
        grid=(ng, nq),
        in_specs=[pl.BlockSpec((C_TQ, wblk), lambda h, i: (i, h)),
                  kv_spec(ng, 2), kv_spec(ng, 1), kv_spec(ng, 0),
                  kv_spec(2 * ng, 2), kv_spec(2 * ng, 1), kv_spec(2 * ng, 0),
                  pl.BlockSpec((C_TQ, wblk), lambda h, i: (i, h)),
                  pl.BlockSpec((None, C_HPS, C_TQ, C_TK), bias_map)],
        out_specs=pl.BlockSpec((C_TQ, wblk), lambda h, i: (i, h)),
        out_shape=jax.ShapeDtypeStruct((s, C_W), out_dtype),
        compiler_params=_params(("parallel", "parallel")),
        name="chunk_attn",
    )(qkv, qkv, qkv, qkv, qkv, qkv, qkv, gate, bias_tbl)


A_TB = 512


def _swap_pairs(x):
    lane = lax.broadcasted_iota(jnp.int32, x.shape, x.ndim - 1)
    nxt = pltpu.roll(x, LANES - 1, x.ndim - 1)
    prv = pltpu.roll(x, 1, x.ndim - 1)
    return jnp.where(lane % 2 == 0, nxt, prv)


def _retention_kernel(q_ref, k_ref, v_ref, g_ref, cos_ref, sin_ref, mask_ref, dq_ref,
                      dk_ref, gb_ref, o_ref, st_ref):
    i = pl.program_id(0)

    @pl.when(i == 0)
    def _():
        st_ref[...] = jnp.zeros_like(st_ref)

    cos = cos_ref[...]
    sin = sin_ref[...]
    hs = range(A_HEADS)

    def rot(ref, h):
        x = ref[:, h * A_DK:(h + 1) * A_DK].astype(F32)
        return x * cos + _swap_pairs(x) * sin

    q = [rot(q_ref, h) for h in hs]
    k = [rot(k_ref, h) * (A_DK ** -0.5) for h in hs]
    v = [v_ref[:, h * A_DV:(h + 1) * A_DV].astype(BF16) for h in hs]
    st = [st_ref[h] for h in hs]
    sc = [_dot_nt(q[h].astype(BF16), k[h].astype(BF16)) * mask_ref[h] for h in hs]
    oi = [_dot((q[h] * dq_ref[h]).astype(BF16), st[h].astype(BF16)) for h in hs]
    o = [_dot(sc[h].astype(BF16), v[h]) + oi[h] for h in hs]
    kv = [_dot_tn((k[h] * dk_ref[h]).astype(BF16), v[h]) for h in hs]
    for h in hs:
        st_ref[h] = gb_ref[h] * st[h] + kv[h]
        on = o[h] * lax.rsqrt(jnp.mean(o[h] * o[h], axis=-1, keepdims=True) + A_NORM_EPS)
        cols = slice(h * A_DV, (h + 1) * A_DV)
        o_ref[:, cols] = (on * _silu(g_ref[:, cols].astype(F32))).astype(o_ref.dtype)


def _retention_tables():
    h = np.arange(A_HEADS, dtype=np.float64)
    log_g = np.log(1.0 - 2.0 ** (-5.0 - h))
    pos = np.arange(A_TB)
    d = pos[:, None] - pos[None, :]
    cq = pos[:, None] // CHUNK
    ck = pos[None, :] // CHUNK
    same = cq == ck
    past = ck < cq
    expo = np.where(same, np.abs(d), np.where(past, d, 0)).astype(np.float64)
    mask = np.exp(log_g[:, None, None] * expo[None]) * (same | past)[None]
    dq = np.exp(log_g[:, None] * (pos + 1.0)[None, :])
    dk = np.exp(log_g[:, None] * (A_TB - 1.0 - pos)[None, :])
    gb = np.exp(log_g * A_TB)
    ones = np.ones((1, 1, LANES))
    return (jnp.asarray(mask, F32), jnp.asarray(dq[:, :, None] * ones, F32),
            jnp.asarray(dk[:, :, None] * ones, F32),
            jnp.asarray(gb[:, None, None] * np.ones((1, 1, A_DV)), F32))


def _rotary_tables(s):
    inv = 1.0 / (ROPE_BASE ** (jnp.arange(0, A_DK, 2, dtype=F32) / A_DK))
    ang = jnp.arange(s).astype(F32)[:, None] * inv[None, :]
    cos = jnp.repeat(jnp.cos(ang), 2, axis=-1)
    sin = jnp.repeat(jnp.sin(ang), 2, axis=-1)
    sign = jnp.asarray(np.tile(np.array([-1.0, 1.0]), A_DK // 2), F32)
    return cos, sin * sign[None, :]


def retention(ha, rot_tbl, ret_tbl, out_dtype):
    s = ha.shape[0]
    cos, sin = rot_tbl
    mask, dq, dk, gb = ret_tbl
    whole = lambda shape: pl.BlockSpec(shape, lambda i: (0,) * len(shape))
    return pl.pallas_call(
        _retention_kernel,
        grid=(s // A_TB,),
        in_specs=[pl.BlockSpec((A_TB, A_QK), lambda i: (i, 0)),
                  pl.BlockSpec((A_TB, A_QK), lambda i: (i, 1)),
                  pl.BlockSpec((A_TB, A_W), lambda i: (i, 2 * A_QK // A_W)),
                  pl.BlockSpec((A_TB, A_W), lambda i: (i, 2 * A_QK // A_W + 1)),
                  pl.BlockSpec((A_TB, A_DK), lambda i: (i, 0)),
                  pl.BlockSpec((A_TB, A_DK), lambda i: (i, 0)),
                  whole((A_HEADS, A_TB, A_TB)), whole((A_HEADS, A_TB, LANES)),
                  whole((A_HEADS, A_TB, LANES)), whole((A_HEADS, 1, A_DV))],
        out_specs=pl.BlockSpec((A_TB, A_W), lambda i: (i, 0)),
        out_shape=jax.ShapeDtypeStruct((s, A_W), out_dtype),
        scratch_shapes=[pltpu.VMEM((A_HEADS, A_DK, A_DV), F32)],
        compiler_params=_params(("arbitrary",)),
        name="retention",
    )(ha, ha, ha, ha, cos, sin, mask, dq, dk, gb)


B_TP = 512
B_TI = 2048
B_NCI = B_TI // CHUNK
B_TC = 512
B_NCB = B_TC // CHUNK
B_NPAIR = B_W // LANES
B_INV_BASE = 16


def _seg_ones():
    r = lax.broadcasted_iota(jnp.int32, (LANES, LANES), 0) // B_HEAD
    c = lax.broadcasted_iota(jnp.int32, (LANES, LANES), 1) // B_HEAD
    return (r == c).astype(BF16)


def _seg_sum(x):
    ones = _seg_ones()
    hi = x.astype(BF16)
    lo = (x - hi.astype(F32)).astype(BF16)
    parts = [_dot(hi[:, j:j + LANES], ones) + _dot(lo[:, j:j + LANES], ones)
             for j in range(0, x.shape[1], LANES)]
    return jnp.concatenate(parts, axis=1)


def _rwkv_prep_kernel(hb_ref, prev_ref, mu_ref, w0_ref, wup_ref, a0_ref, aup_ref, kk_ref,
                      ka_ref, rk_ref, rt_ref, kq_ref, bt_ref, kt_ref, v_ref, bte_ref,
                      kte_ref, bonus_ref, pl_ref):
    i = pl.program_id(0)
    hb = hb_ref[...]
    prev_row = jnp.where(i == 0, 0.0, prev_ref[SUBLANES - 1:SUBLANES, :])
    rolled = pltpu.roll(hb, 1, 0)
    row = lax.broadcasted_iota(jnp.int32, (SUBLANES, hb.shape[1]), 0)
    shifted = jnp.concatenate([jnp.where(row == 0, prev_row, rolled[0:SUBLANES]),
                               rolled[SUBLANES:]], axis=0)
    xs = hb + (shifted - hb) * mu_ref[...]
    r = xs[:, 0:B_W]
    k = xs[:, B_W:2 * B_W]
    v = xs[:, 2 * B_W:3 * B_W]
    wd = xs[:, 3 * B_W:3 * B_W + B_RANK]
    ad = xs[:, 3 * B_W + B_RANK:3 * B_W + 2 * B_RANK]
    z = w0_ref[...] + _mm(jnp.tanh(wd), wup_ref[...])
    lw = -math.exp(-0.5) * jax.nn.sigmoid(z)
    a = jax.nn.sigmoid(a0_ref[...] + _mm(ad, aup_ref[...]))
    kk = k * kk_ref[...]
    kk = kk * lax.rsqrt(jnp.maximum(_seg_sum(kk * kk), 1e-24))
    kmod = k * (1.0 + (a - 1.0) * ka_ref[...])
    bonus_ref[...] = _seg_sum(r * kmod * rk_ref[...]) * v
    tr = lax.broadcasted_iota(jnp.int32, (B_TP, B_TP), 0)
    tc = lax.broadcasted_iota(jnp.int32, (B_TP, B_TP), 1)
    tri = ((tr // CHUNK == tc // CHUNK) & (tr >= tc)).astype(BF16)
    hi = lw.astype(BF16)
    rest = lw - hi.astype(F32)
    mid = rest.astype(BF16)
    lo = (rest - mid.astype(F32)).astype(BF16)
    cum = _dot(tri, hi) + _dot(tri, mid) + _dot(tri, lo)
    nc = B_TP // CHUNK
    ends = [cum[(c + 1) * CHUNK - 1:(c + 1) * CHUNK, :] for c in range(nc)]
    p_inv = jnp.exp(-cum)
    p_end = jnp.concatenate([jnp.exp(ends[c] - cum[c * CHUNK:(c + 1) * CHUNK, :])
                             for c in range(nc)], axis=0)
    rt_ref[...] = (r * jnp.exp(cum)).astype(BF16)
    kq_ref[...] = (kk * jnp.exp(cum - lw)).astype(BF16)
    bt_ref[...] = (kk * a * p_inv).astype(BF16)
    kt_ref[...] = (kmod * p_inv).astype(BF16)
    v_ref[...] = v.astype(BF16)
    bte_ref[...] = (kk * a * p_end).astype(BF16)
    kte_ref[...] = (kmod * p_end).astype(BF16)
    pl_ref[...] = jnp.exp(jnp.concatenate(ends, axis=0))


def rwkv_prep(hb, mu, w0, w_up, a0, a_up, k_k, k_a, r_k):
    s = hb.shape[0]
    nb = s // B_TP
    vec = lambda n: pl.BlockSpec((1, n), lambda i: (0, 0))
    full = lambda a, b: pl.BlockSpec((a, b), lambda i: (0, 0))
    big = pl.BlockSpec((B_TP, B_W), lambda i: (i, 0))
    outs = ([jax.ShapeDtypeStruct((s, B_W), BF16)] * 7
            + [jax.ShapeDtypeStruct((s, B_W), F32),
               jax.ShapeDtypeStruct((s // CHUNK, B_W), F32)])
    return pl.pallas_call(
        _rwkv_prep_kernel,
        grid=(nb,),
        in_specs=[pl.BlockSpec((B_TP, B_SHIFT_W), lambda i: (i, 0)),
                  pl.BlockSpec((SUBLANES, B_SHIFT_W),
                               lambda i: (jnp.maximum(i * (B_TP // SUBLANES) - 1, 0), 0)),
                  vec(B_SHIFT_W), vec(B_W), full(B_RANK, B_W), vec(B_W), full(B_RANK, B_W),
                  vec(B_W), vec(B_W), vec(B_W)],
        out_specs=[big] * 8 + [pl.BlockSpec((B_TP // CHUNK, B_W), lambda i: (i, 0))],
        out_shape=outs,
        compiler_params=_params(("parallel",)),
        name="rwkv_prep",
    )(hb, hb, mu.reshape(1, -1), w0.reshape(1, -1), w_up, a0.reshape(1, -1), a_up,
      k_k.reshape(1, -1), k_a.reshape(1, -1), r_k.reshape(1, -1))


def _stack_heads(x):
    lane = lax.broadcasted_iota(jnp.int32, x.shape, 1)
    zero = jnp.zeros_like(x)
    return jnp.concatenate([jnp.where(lane < B_HEAD, x, zero),
                            jnp.where(lane >= B_HEAD, x, zero)], axis=0)


def _fold_heads(x):
    return x[:CHUNK] + x[CHUNK:]


def _mm(a, b):
    return _dot(a.astype(BF16), b.astype(BF16))


def _rwkv_intra_kernel(rt_ref, kq_ref, bt_ref, kt_ref, v_ref, bte_ref, kte_ref, pl_ref,
                       mz_ref, gz_ref, rq_ref, yc_ref):
    n2 = 2 * CHUNK
    r_i = lax.broadcasted_iota(jnp.int32, (n2, n2), 0)
    c_i = lax.broadcasted_iota(jnp.int32, (n2, n2), 1)
    same_head = (r_i // CHUNK) == (c_i // CHUNK)
    strict = (same_head & (r_i > c_i)).astype(F32)
    incl = (same_head & (r_i >= c_i)).astype(F32)
    eye = (r_i == c_i).astype(F32)
    bd16 = ((r_i // B_INV_BASE) == (c_i // B_INV_BASE)).astype(F32)
    bd32 = ((r_i // (2 * B_INV_BASE)) == (c_i // (2 * B_INV_BASE))).astype(F32)

    cs = range(B_NCI)

    def load(ref):
        return [_stack_heads(ref[c * CHUNK:(c + 1) * CHUNK, :]) for c in cs]

    def each(fn, *lists):
        return [fn(*args) for args in zip(*lists)]

    rt, kq, bt, kt, vv, bte, kte = (load(r) for r in (rt_ref, kq_ref, bt_ref, kt_ref, v_ref,
                                                      bte_ref, kte_ref))
    ab = each(lambda kq_, rt_, bt_, kt_: _dot_nt(jnp.concatenate([kq_, rt_], axis=0),
                                                 jnp.concatenate([bt_, kt_], axis=0)),
              kq, rt, bt, kt)
    a_m = [x[:n2, :n2] * strict for x in ab]
    b_m = [x[:n2, n2:] * strict for x in ab]
    aq = [x[n2:, :n2] * incl for x in ab]
    bq = [x[n2:, n2:] * incl for x in ab]
    nn = [-(x * bd16) for x in a_m]
    t = [eye + x for x in nn]
    sq = each(_mm, nn, nn)
    doublings = B_INV_BASE.bit_length() - 2
    for step in range(doublings):
        t = each(lambda t_, s_: t_ + _mm(t_, s_), t, sq)
        if step + 1 < doublings:
            sq = each(_mm, sq, sq)
    for blk in (bd32 - bd16, 1.0 - bd32):
        e = each(lambda a_, t_: _mm(a_ * blk, t_), a_m, t)
        t = each(lambda t_, e_: t_ - _mm(t_, e_), t, e)
    bv = each(_mm, b_m, vv)
    tw = each(lambda t_, kq_, bv_: _mm(t_, jnp.concatenate([kq_.astype(F32), bv_], axis=1)),
              t, kq, bv)
    w = [x[:, :n2].astype(BF16) for x in tw]
    u0 = [(-x[:, n2:]).astype(BF16) for x in tw]
    aw = each(lambda aq_, w_, u_: _mm(aq_, jnp.concatenate([w_, u_], axis=1)), aq, w, u0)
    bqv = each(_mm, bq, vv)
    wb = each(_dot_tn, w, bte)
    gz = each(lambda u_, v_, b_, k_: _dot_tn(jnp.concatenate([u_, v_], axis=0),
                                             jnp.concatenate([b_, k_], axis=0)),
              u0, vv, bte, kte)
    for c in cs:
        rq_ref[0, c] = _fold_heads(rt[c].astype(F32) - aw[c][:, :n2]).astype(BF16)
        yc_ref[c * CHUNK:(c + 1) * CHUNK, :] = _fold_heads(bqv[c] + aw[c][:, n2:])
        mz_ref[0, c] = _fold_heads(eye * pl_ref[c:c + 1, :] - wb[c]).astype(BF16)
        gz_ref[0, c] = _fold_heads(gz[c])


def rwkv_intra(prep):
    rt, kq, bt, kt, v, bte, kte, _, p_end = prep
    s = rt.shape[0]
    nchunk = s // CHUNK
    blk = pl.BlockSpec((B_TI, LANES), lambda p, i: (i, p))
    mat = pl.BlockSpec((1, B_NCI, CHUNK, LANES), lambda p, i: (p, i, 0, 0))
    mshape = (B_NPAIR, nchunk, CHUNK, LANES)
    return pl.pallas_call(
        _rwkv_intra_kernel,
        grid=(B_NPAIR, s // B_TI),
        in_specs=[blk] * 7 + [pl.BlockSpec((B_NCI, LANES), lambda p, i: (i, p))],
        out_specs=[mat, mat, mat, blk],
        out_shape=[jax.ShapeDtypeStruct(mshape, BF16), jax.ShapeDtypeStruct(mshape, F32),
                   jax.ShapeDtypeStruct(mshape, BF16), jax.ShapeDtypeStruct((s, B_W), F32)],
        compiler_params=_params(("parallel", "parallel")),
        name="rwkv_intra",
    )(rt, kq, bt, kt, v, bte, kte, p_end)


def _rwkv_serial_kernel(mz_ref, gz_ref, rq_ref, yc_ref, bonus_ref, g_ref, lng_ref, lnb_ref,
                        o_ref, zt_ref, y_ref):
    i = pl.program_id(0)

    @pl.when(i == 0)
    def _():
        zt_ref[...] = jnp.zeros_like(zt_ref)

    zt = [zt_ref[p] for p in range(B_NPAIR)]
    for c in range(B_NCB):
        rows = slice(c * CHUNK, (c + 1) * CHUNK)
        zb = [z.astype(BF16) for z in zt]
        y2 = [_dot_nt(_stack_heads(rq_ref[p, c]), zb[p]) for p in range(B_NPAIR)]
        zt = [_dot(zb[p], _stack_heads(mz_ref[p, c])) + _stack_heads(gz_ref[p, c])
              for p in range(B_NPAIR)]
        for p in range(B_NPAIR):
            cols = slice(p * LANES, (p + 1) * LANES)
            y_ref[rows, cols] = y2[p][:CHUNK] + y2[p][CHUNK:] + yc_ref[rows, cols]
    for p in range(B_NPAIR):
        zt_ref[p] = zt[p]
    y = y_ref[...]
    mean = _seg_sum(y) * (1.0 / B_HEAD)
    yc = y - mean
    var = _seg_sum(yc * yc) * (1.0 / B_HEAD)
    yn = yc * lax.rsqrt(var + B_LN_EPS) * lng_ref[...] + lnb_ref[...]
    o_ref[...] = ((yn + bonus_ref[...]) * _silu(g_ref[...].astype(F32))).astype(o_ref.dtype)


def rwkv_serial(intra, bonus, gate, ln_g, ln_b, out_dtype):
    mz, gz, rq, yc = intra
    s = yc.shape[0]
    mat = pl.BlockSpec((B_NPAIR, B_NCB, CHUNK, LANES), lambda i: (0, i, 0, 0))
    blk = pl.BlockSpec((B_TC, B_W), lambda i: (i, 0))
    vec = pl.BlockSpec((1, B_W), lambda i: (0, 0))
    return pl.pallas_call(
        _rwkv_serial_kernel,
        grid=(s // B_TC,),
        in_specs=[mat, mat, mat, blk, blk, blk, vec, vec],
        out_specs=blk,
        out_shape=jax.ShapeDtypeStruct((s, B_W), out_dtype),
        scratch_shapes=[pltpu.VMEM((B_NPAIR, LANES, LANES), F32),
                        pltpu.VMEM((B_TC, B_W), F32)],
        compiler_params=_params(("arbitrary",)),
        name="rwkv_serial",
    )(mz, gz, rq, yc, bonus, gate, ln_g.reshape(1, -1), ln_b.reshape(1, -1))


def rwkv7(hb, gate, mu, w0, w_up, a0, a_up, k_k, k_a, r_k, ln_g, ln_b, out_dtype):
    prep = rwkv_prep(hb, mu, w0, w_up, a0, a_up, k_k, k_a, r_k)
    intra = rwkv_intra(prep)
    return rwkv_serial(intra, prep[7], gate, ln_g, ln_b, out_dtype)


def kernel(x, norm_g, w_in, w_out, b_mu, b_w0, b_w_up, b_a0, b_a_up, b_k_k, b_k_a, b_r_k,
           b_ln_g, b_ln_b, c_rel_bias, final_g):
    bsz, s, d = x.shape
    assert bsz == 1 and d == D_MODEL
    assert all(s % rows == 0 for rows in (P_TM, B_TI, B_TP, B_TC, A_TB, C_TQ))
    xs = x.reshape(s, d)
    rot_tbl = _rotary_tables(s)
    ret_tbl = _retention_tables()
    bias_tbl = _attn_bias_tables(c_rel_bias)
    xg, ssq = prenorm(xs, norm_g[0])
    for l in range(DEPTH):
        ha, hb, gb, qkv, gc = inproj(xg, ssq, w_in, l)
        ya = retention(ha, rot_tbl, ret_tbl, BF16)
        yb = rwkv7(hb, gb, b_mu[l], b_w0[l], b_w_up[l], b_a0[l], b_a_up[l], b_k_k[l],
                   b_k_a[l], b_r_k[l].reshape(-1), b_ln_g[l], b_ln_b[l], BF16)
        yc = chunk_attention(qkv, gc, bias_tbl, l, BF16)
        if l + 1 < DEPTH:
            xs, xg, ssq = outproj(ya, yb, yc, w_out, l, xs, norm_g[l + 1])
        else:
            xs = outproj(ya, yb, yc, w_out, l, xs)
    return rmsnorm(xs, final_g, F32).reshape(bsz, s, d)
```
